```python
import math
import jax, jax.numpy as jnp
from jax import lax
import numpy as np

D_MODEL = 1024
BATCH = 16
SEQ = 4096
DEPTH = 4

CTX_LEN = 256
GRID_W = 64
N_MIXERS = 3
EPS = 1e-6
D_FF = 4 * D_MODEL
ATTN_HEADS = 8
ATTN_HD = D_MODEL // ATTN_HEADS // 2
ATTN_VD = 2 * ATTN_HD
QK_COLS = ATTN_HEADS * 2 * ATTN_HD
V_COLS = ATTN_HEADS * ATTN_VD
Q_BLOCK = 128
ROPE_BASE = 10000.0
S5_GROUP = 16
S5_GROUPS = D_MODEL // S5_GROUP
S5_STATE = 64
S5_CHUNK = 128
LRU_BW = 256
LRU_WIDTH = ((4 * D_MODEL // 3 + LRU_BW - 1) // LRU_BW) * LRU_BW
LRU_BLOCKS = LRU_WIDTH // LRU_BW
LRU_C = 8.0
CONV_W = 4
CONV_LEFT = 2

kernel_name = "hybrid_diffattn_s5_rglru_ctxprefix"


def rmsnorm(x, g):
    xf = x.astype(jnp.float32)
    y = xf * lax.rsqrt(jnp.mean(xf * xf, axis=-1, keepdims=True) + EPS)
    return (y * g.astype(jnp.float32)).astype(x.dtype)


def _linear_combine(left, right):
    a_l, b_l = left
    a_r, b_r = right
    return a_l * a_r, a_r * b_l + b_r


def axial_rope(row, col):
    n_freq = ATTN_HD // 4
    inv = ROPE_BASE ** (-jnp.arange(n_freq, dtype=jnp.float32) / n_freq)
    ang = jnp.concatenate([row[:, None].astype(jnp.float32) * inv,
                           col[:, None].astype(jnp.float32) * inv], axis=-1)
    return jnp.cos(ang), jnp.sin(ang)


def apply_rope(t, cos, sin):
    half = ATTN_HD // 2
    tf = t.astype(jnp.float32)
    t1, t2 = tf[..., :half], tf[..., half:]
    cs = cos[:, None, None, :]
    sn = sin[:, None, None, :]
    out = jnp.concatenate([t1 * cs - t2 * sn, t1 * sn + t2 * cs], axis=-1)
    return out.astype(t.dtype)


def diff_attention(h_lat, h_ctx, w_qkv, w_o, lam_vecs, subln_g, lambda_init, rope, need_ctx):
    bsz, seq, _ = h_lat.shape
    cos, sin = rope
    scale = ATTN_HD ** -0.5

    def heads_qk(t):
        return t.reshape(t.shape[0], t.shape[1], ATTN_HEADS, 2, ATTN_HD)

    def heads_v(t):
        return t.reshape(t.shape[0], t.shape[1], ATTN_HEADS, ATTN_VD)

    q_l, k_l, v_l = jnp.split(jnp.einsum('btd,de->bte', h_lat, w_qkv), [QK_COLS, 2 * QK_COLS], axis=-1)
    q_l = apply_rope(heads_qk(q_l), cos, sin) * scale
    k_l = apply_rope(heads_qk(k_l), cos, sin)
    k_c, v_c = jnp.split(jnp.einsum('btd,de->bte', h_ctx, w_qkv[:, QK_COLS:]), [QK_COLS], axis=-1)
    k_c, v_c = heads_qk(k_c), heads_v(v_c)
    k_all = jnp.concatenate([k_c, k_l], axis=1)
    v_all = jnp.concatenate([v_c, heads_v(v_l)], axis=1)

    lf = lam_vecs.astype(jnp.float32)
    lam = jnp.exp(jnp.sum(lf[0] * lf[1])) - jnp.exp(jnp.sum(lf[2] * lf[3])) + lambda_init

    def attend(q, k, v):
        s = jnp.einsum('bqhcd,bkhcd->bhcqk', q, k).astype(jnp.float32)
        p = jax.nn.softmax(s, axis=-1)
        w = p[:, :, 0] - lam * p[:, :, 1]
        o = jnp.einsum('bhqk,bkhe->bqhe', w.astype(v.dtype), v)
        return rmsnorm(o, subln_g) * (1.0 - lambda_init)

    n_blocks = seq // Q_BLOCK
    qb = jnp.moveaxis(q_l.reshape(bsz, n_blocks, Q_BLOCK, ATTN_HEADS, 2, ATTN_HD), 1, 0)
    o_l = lax.map(lambda qq: attend(qq, k_all, v_all), qb)
    o_l = jnp.moveaxis(o_l, 0, 1).reshape(bsz, seq, V_COLS)
    y_lat = jnp.einsum('bte,ed->btd', o_l, w_o)
    y_ctx = None
    if need_ctx:
        q_c = heads_qk(jnp.einsum('btd,de->bte', h_ctx, w_qkv[:, :QK_COLS])) * scale
        o_c = attend(q_c, k_c, v_c).reshape(bsz, h_ctx.shape[1], V_COLS)
        y_ctx = jnp.einsum('bte,ed->btd', o_c, w_o)
    return y_lat, y_ctx


def s5_discretize(a_re, a_im, b_re, b_im, log_dt):
    lam = lax.complex(a_re.astype(jnp.float32), a_im.astype(jnp.float32))
    dt = jnp.exp(log_dt.astype(jnp.float32))[:, None]
    lam_bar = jnp.exp(lam * dt)
    b_mat = lax.complex(b_re.astype(jnp.float32), b_im.astype(jnp.float32))
    b_bar = ((lam_bar - 1.0) / lam)[..., None] * b_mat
    return lam_bar, b_bar


def s5_scan(u, lam_bar, b_bar, c_mat, h0):
    bsz, t_len, _ = u.shape
    n_chunks = t_len // S5_CHUNK
    ug = u.astype(jnp.float32).reshape(bsz, n_chunks, S5_CHUNK, S5_GROUPS, S5_GROUP)
    ug = jnp.moveaxis(ug, 1, 0)
    a = jnp.broadcast_to(lam_bar, (bsz, S5_CHUNK, S5_GROUPS, S5_STATE))

    def step(h, uc):
        bu = jnp.einsum('gpc,btgc->btgp', b_bar, uc.astype(jnp.complex64))
        a_cum, hs = lax.associative_scan(_linear_combine, (a, bu), axis=1)
        hs = hs + a_cum * h[:, None]
        y = jnp.real(jnp.einsum('gcp,btgp->btgc', c_mat, hs))
        return hs[:, -1], y

    h_last, ys = lax.scan(step, h0, ug)
    return jnp.moveaxis(ys, 0, 1).reshape(bsz, t_len, D_MODEL), h_last


def s5_mixer(h_lat, h_ctx, a_re, a_im, b_re, b_im, c_re, c_im, log_dt, d_skip, w_glu, need_ctx):
    bsz = h_lat.shape[0]
    lam_f, b_f = s5_discretize(a_re[0], a_im[0], b_re[0], b_im[0], log_dt[0])
    lam_b, b_b = s5_discretize(a_re[1], a_im[1], b_re[1], b_im[1], log_dt[1])
    c_f = lax.complex(c_re[0].astype(jnp.float32), c_im[0].astype(jnp.float32))
    c_b = lax.complex(c_re[1].astype(jnp.float32), c_im[1].astype(jnp.float32))
    zeros = jnp.zeros((bsz, S5_GROUPS, S5_STATE), jnp.complex64)
    y_cf, h_cf = s5_scan(h_ctx, lam_f, b_f, c_f, zeros)
    y_cb, h_cb = s5_scan(h_ctx[:, ::-1], lam_b, b_b, c_b, zeros)
    y_lf, _ = s5_scan(h_lat, lam_f, b_f, c_f, h_cf)
    y_lb, _ = s5_scan(h_lat[:, ::-1], lam_b, b_b, c_b, h_cb)
    d32 = d_skip.astype(jnp.float32)

    def glu(y, h):
        g = jax.nn.gelu(y.astype(h.dtype))
        o1, o2 = jnp.split(jnp.einsum('btd,de->bte', g, w_glu), 2, axis=-1)
        return o1 * jax.nn.sigmoid(o2)

    y_lat = glu(y_lf + y_lb[:, ::-1] + d32 * h_lat.astype(jnp.float32), h_lat)
    y_ctx = glu(y_cf + y_cb[:, ::-1] + d32 * h_ctx.astype(jnp.float32), h_ctx) if need_ctx else None
    return y_lat, y_ctx


def depthwise_conv(u, w, b):
    out = lax.conv_general_dilated(u, w[:, None, :], window_strides=(1,),
                                   padding=[(CONV_LEFT, CONV_W - 1 - CONV_LEFT)],
                                   dimension_numbers=('NWC', 'WIO', 'NWC'),
                                   feature_group_count=u.shape[-1])
    return out + b


def rglru_mixer(h_lat, h_ctx, w_in, conv_w, conv_b, w_gate, b_gate, a_param, w_out, need_ctx):
    bsz = h_lat.shape[0]

    def branches(h):
        gate_in, rec_in = jnp.split(jnp.einsum('btd,de->bte', h, w_in), 2, axis=-1)
        return jax.nn.gelu(gate_in), depthwise_conv(rec_in, conv_w, conv_b)

    def gate_terms(u, d):
        t_len = u.shape[1]
        ub = u.reshape(bsz, t_len, LRU_BLOCKS, LRU_BW)
        g = jnp.einsum('btnh,knhe->kbtne', ub, w_gate[d]).reshape(2, bsz, t_len, LRU_WIDTH)
        g = g.astype(jnp.float32) + b_gate[d][:, None, None, :].astype(jnp.float32)
        r = jax.nn.sigmoid(g[0])
        ig = jax.nn.sigmoid(g[1])
        log_a = -LRU_C * r * jax.nn.softplus(-a_param[d].astype(jnp.float32))
        a = jnp.exp(log_a)
        b = jnp.sqrt(-jnp.expm1(2.0 * log_a)) * (ig * u.astype(jnp.float32))
        return a, b

    def run(ab, h0):
        a_cum, hs = lax.associative_scan(_linear_combine, ab, axis=1)
        return hs + a_cum * h0[:, None, :]

    g_c, u_c = branches(h_ctx)
    g_l, u_l = branches(h_lat)
    zeros = jnp.zeros((bsz, LRU_WIDTH), jnp.float32)
    hc_f = run(gate_terms(u_c, 0), zeros)
    hc_b = run(gate_terms(u_c[:, ::-1], 1), zeros)
    hl_f = run(gate_terms(u_l, 0), hc_f[:, -1])
    hl_b = run(gate_terms(u_l[:, ::-1], 1), hc_b[:, -1])
    y_lat = jnp.einsum('bte,ed->btd', (hl_f + hl_b[:, ::-1]).astype(g_l.dtype) * g_l, w_out)
    y_ctx = None
    if need_ctx:
        y_ctx = jnp.einsum('bte,ed->btd', (hc_f + hc_b[:, ::-1]).astype(g_c.dtype) * g_c, w_out)
    return y_lat, y_ctx


def sqrelu_mlp(h, w1, w2):
    return jnp.einsum('btf,fd->btd', jnp.square(jax.nn.relu(jnp.einsum('btd,df->btf', h, w1))), w2)


def setup_inputs(seed: int = 0) -> dict:
    key = jax.random.key(seed)
    ks = iter(jax.random.split(key, 40))
    f32 = jnp.float32

    def nrm(shape, scale):
        return jax.random.normal(next(ks), shape, f32) * scale

    n_a = len(range(0, DEPTH, N_MIXERS))
    n_b = len(range(1, DEPTH, N_MIXERS))
    n_c = len(range(2, DEPTH, N_MIXERS))
    x = nrm((BATCH, SEQ, D_MODEL), 1.0)
    c = nrm((BATCH, D_MODEL), 1.0)
    ctx = nrm((BATCH, CTX_LEN, D_MODEL), 1.0)
    c_ctx = nrm((D_MODEL,), 1.0)
    ada_w = nrm((DEPTH, D_MODEL, 6 * D_MODEL), 0.5 * D_MODEL ** -0.5)
    ada_b = nrm((DEPTH, 6 * D_MODEL), 0.01)
    norm_g = 1.0 + nrm((DEPTH, 4, D_MODEL), 0.01)
    mlp_w1 = nrm((DEPTH, D_MODEL, D_FF), D_MODEL ** -0.5)
    mlp_w2 = nrm((DEPTH, D_FF, D_MODEL), D_FF ** -0.5)
    attn_w_qkv = nrm((n_a, D_MODEL, 2 * QK_COLS + V_COLS), D_MODEL ** -0.5)
    attn_w_o = nrm((n_a, V_COLS, D_MODEL), V_COLS ** -0.5)
    attn_lambda = nrm((n_a, 4, ATTN_HD), 0.1)
    attn_subln = 1.0 + nrm((n_a, ATTN_VD), 0.01)
    s5_a_re = -0.5 + nrm((n_b, 2, S5_GROUPS, S5_STATE), 0.01)
    s5_a_im = jnp.pi * jnp.arange(S5_STATE, dtype=f32) * jnp.ones((n_b, 2, S5_GROUPS, 1), f32)
    s5_b_re = nrm((n_b, 2, S5_GROUPS, S5_STATE, S5_GROUP), (2 * S5_GROUP) ** -0.5)
    s5_b_im = nrm((n_b, 2, S5_GROUPS, S5_STATE, S5_GROUP), (2 * S5_GROUP) ** -0.5)
    s5_c_re = nrm((n_b, 2, S5_GROUPS, S5_GROUP, S5_STATE), S5_STATE ** -0.5)
    s5_c_im = nrm((n_b, 2, S5_GROUPS, S5_GROUP, S5_STATE), S5_STATE ** -0.5)
    s5_log_dt = jax.random.uniform(next(ks), (n_b, 2, S5_GROUPS), f32, math.log(1e-3), math.log(1e-1))
    s5_d = nrm((n_b, D_MODEL), 1.0)
    s5_w_glu = nrm((n_b, D_MODEL, 2 * D_MODEL), D_MODEL ** -0.5)
    lru_w_in = nrm((n_c, D_MODEL, 2 * LRU_WIDTH), D_MODEL ** -0.5)
    lru_conv_w = nrm((n_c, CONV_W, LRU_WIDTH), CONV_W ** -0.5)
    lru_conv_b = nrm((n_c, LRU_WIDTH), 0.01)
    lru_w_gate = nrm((n_c, 2, 2, LRU_BLOCKS, LRU_BW, LRU_BW), LRU_BW ** -0.5)
    lru_b_gate = nrm((n_c, 2, 2, LRU_WIDTH), 0.01)
    a0 = jax.random.uniform(next(ks), (n_c, 2, LRU_WIDTH), f32, 0.9, 0.999)
    s = a0 ** (1.0 / LRU_C)
    lru_a_param = jnp.log(s) - jnp.log1p(-s)
    lru_w_out = nrm((n_c, LRU_WIDTH, D_MODEL), LRU_WIDTH ** -0.5)
    return {"x": x, "c": c, "ctx": ctx, "c_ctx": c_ctx,
            "ada_w": ada_w, "ada_b": ada_b, "norm_g": norm_g, "mlp_w1": mlp_w1, "mlp_w2": mlp_w2,
            "attn_w_qkv": attn_w_qkv, "attn_w_o": attn_w_o, "attn_lambda": attn_lambda, "attn_subln": attn_subln,
            "s5_a_re": s5_a_re, "s5_a_im": s5_a_im, "s5_b_re": s5_b_re, "s5_b_im": s5_b_im,
            "s5_c_re": s5_c_re, "s5_c_im": s5_c_im, "s5_log_dt": s5_log_dt, "s5_d": s5_d, "s5_w_glu": s5_w_glu,
            "lru_w_in": lru_w_in, "lru_conv_w": lru_conv_w, "lru_conv_b": lru_conv_b, "lru_w_gate": lru_w_gate,
            "lru_b_gate": lru_b_gate, "lru_a_param": lru_a_param, "lru_w_out": lru_w_out}


def reference(x, c, ctx, c_ctx, ada_w, ada_b, norm_g, mlp_w1, mlp_w2,
              attn_w_qkv, attn_w_o, attn_lambda, attn_subln,
              s5_a_re, s5_a_im, s5_b_re, s5_b_im, s5_c_re, s5_c_im, s5_log_dt, s5_d, s5_w_glu,
              lru_w_in, lru_conv_w, lru_conv_b, lru_w_gate, lru_b_gate, lru_a_param, lru_w_out):
    seq = x.shape[1]
    rows = seq // GRID_W
    row = jnp.repeat(jnp.arange(rows), GRID_W)
    col = jnp.tile(jnp.arange(GRID_W), rows)
    rope = axial_rope(row, col)
    xc = ctx
    sc = jax.nn.silu(c)
    scc = jax.nn.silu(c_ctx)
    for i in range(DEPTH):
        need_ctx = i < DEPTH - 1
        mod_l = (jnp.einsum('bd,de->be', sc, ada_w[i]) + ada_b[i])[:, None, :]
        mod_c = jnp.einsum('d,de->e', scc, ada_w[i]) + ada_b[i]
        sh1, s1, g1, sh2, s2, g2 = jnp.split(mod_l, 6, axis=-1)
        csh1, cs1, cg1, csh2, cs2, cg2 = jnp.split(mod_c, 6, axis=-1)
        h_lat = rmsnorm(x, norm_g[i, 0]) * (1.0 + s1) + sh1
        h_ctx = rmsnorm(xc, norm_g[i, 0]) * (1.0 + cs1) + csh1
        kind, j = i % N_MIXERS, i // N_MIXERS
        if kind == 0:
            lambda_init = 0.8 - 0.6 * math.exp(-0.3 * i)
            y_lat, y_ctx = diff_attention(h_lat, h_ctx, attn_w_qkv[j], attn_w_o[j], attn_lambda[j],
                                          attn_subln[j], lambda_init, rope, need_ctx)
        elif kind == 1:
            y_lat, y_ctx = s5_mixer(h_lat, h_ctx, s5_a_re[j], s5_a_im[j], s5_b_re[j], s5_b_im[j],
                                    s5_c_re[j], s5_c_im[j], s5_log_dt[j], s5_d[j], s5_w_glu[j], need_ctx)
        else:
            y_lat, y_ctx = rglru_mixer(h_lat, h_ctx, lru_w_in[j], lru_conv_w[j], lru_conv_b[j], lru_w_gate[j],
                                       lru_b_gate[j], lru_a_param[j], lru_w_out[j], need_ctx)
        x = x + g1 * rmsnorm(y_lat, norm_g[i, 1])
        h = rmsnorm(x, norm_g[i, 2]) * (1.0 + s2) + sh2
        x = x + g2 * rmsnorm(sqrelu_mlp(h, mlp_w1[i], mlp_w2[i]), norm_g[i, 3])
        if need_ctx:
            xc = xc + cg1 * rmsnorm(y_ctx, norm_g[i, 1])
            hc = rmsnorm(xc, norm_g[i, 2]) * (1.0 + cs2) + csh2
            xc = xc + cg2 * rmsnorm(sqrelu_mlp(hc, mlp_w1[i], mlp_w2[i]), norm_g[i, 3])
    return x
```

```python
import functools
import math

import jax
import jax.numpy as jnp
from jax import lax
from jax.experimental import pallas as pl
from jax.experimental.pallas import tpu as pltpu

EPS = 1e-6
GRID_W = 64
ROPE_BASE = 10000.0
N_MIXERS = 3
ATTN_HEADS = 8
S5_GROUP = 16
S5_STATE = 64
LRU_BW = 256
LRU_C = 8.0
CONV_W = 4
CONV_LEFT = 2

LANES = 128
SUBLANES = 8
VMEM_LIMIT = 56 * 1024 * 1024

_BF = jnp.bfloat16
_F32 = jnp.float32


def _cparams(n_axes):
    return pltpu.CompilerParams(dimension_semantics=("arbitrary",) * n_axes,
                                vmem_limit_bytes=VMEM_LIMIT)


def _rms(x):
    return x * lax.rsqrt(jnp.mean(x * x, axis=-1, keepdims=True) + EPS)


def _normmod(x3, gain, scale, shift):
    return _rms(x3) * gain * (1.0 + scale) + shift


def _dot(a, b):
    return jnp.dot(a, b, preferred_element_type=_F32)


def _ada_kernel(c_ref, w_ref, b_ref, o_ref):
    sc = jax.nn.silu(c_ref[...]).astype(_BF)
    o_ref[0] = _dot(sc, w_ref[0]) + b_ref[0]


def _ada_call(c_all, ada_w, ada_b):
    depth, d, n = ada_w.shape
    rows = c_all.shape[0]
    tn = n // 4
    return pl.pallas_call(
        _ada_kernel,
        out_shape=jax.ShapeDtypeStruct((depth, rows, n), _F32),
        grid=(depth, n // tn),
        in_specs=[pl.BlockSpec((rows, d), lambda l, j: (0, 0)),
                  pl.BlockSpec((1, d, tn), lambda l, j: (l, 0, j)),
                  pl.BlockSpec((1, 1, tn), lambda l, j: (l, 0, j))],
        out_specs=pl.BlockSpec((1, rows, tn), lambda l, j: (l, 0, j)),
        compiler_params=_cparams(2),
        name="adaln",
    )(c_all, ada_w, ada_b.reshape(depth, 1, n))


def _rope_cols(t, cos, sin, lo_half):
    rot = jnp.where(lo_half, pltpu.roll(t, 96, 1), pltpu.roll(t, 32, 1))
    return t * cos + rot * sin


def _qkv_kernel(x_ref, mod_ref, g_ref, w_ref, cos_ref, sin_ref, o_ref, *, d, q_scale):
    tt, nb, _ = x_ref.shape
    tm = tt * nb
    mod = mod_ref[0]
    h = _normmod(x_ref[...], g_ref[...], mod[:, d:2 * d], mod[:, 0:d])
    hb = h.reshape(tm, d).astype(_BF)
    cos = cos_ref[...]
    sin = sin_ref[...]
    lane = lax.broadcasted_iota(jnp.int32, (tm, LANES), 1)
    lo_half = (lane % 64) < 32
    n_rope = 2 * d // LANES
    cw = 4 * LANES
    for c0 in range(0, 3 * d, cw):
        acc = _dot(hb, w_ref[:, c0:c0 + cw])
        for k in range(cw // LANES):
            col = c0 // LANES + k
            t = acc[:, k * LANES:(k + 1) * LANES]
            if col < n_rope:
                t = _rope_cols(t, cos, sin, lo_half)
                if col < n_rope // 2:
                    t = t * q_scale
            o_ref[:, col * LANES:(col + 1) * LANES] = t.astype(o_ref.dtype)


def _qkv_call(x3, modtab, gain, w, cos_t, sin_t, *, tt, ctx_len):
    ta, nb, d = x3.shape
    tm = tt * nb
    n = w.shape[1]
    nctx = ctx_len // tt
    kern = functools.partial(_qkv_kernel, d=d, q_scale=(d // ATTN_HEADS // 2) ** -0.5)
    return pl.pallas_call(
        kern,
        out_shape=jax.ShapeDtypeStruct((ta * nb, n), _BF),
        grid=(ta // tt,),
        in_specs=[pl.BlockSpec((tt, nb, d), lambda i: (i, 0, 0)),
                  pl.BlockSpec((1, nb, 6 * d), lambda i: (jnp.where(i < nctx, 0, 1), 0, 0)),
                  pl.BlockSpec((1, d), lambda i: (0, 0)),
                  pl.BlockSpec((d, n), lambda i: (0, 0)),
                  pl.BlockSpec((tm, LANES), lambda i: (i, 0)),
                  pl.BlockSpec((tm, LANES), lambda i: (i, 0))],
        out_specs=pl.BlockSpec((tm, n), lambda i: (i, 0)),
        compiler_params=_cparams(1),
        name="qkv_rope",
    )(x3, modtab, gain, w, cos_t, sin_t)


def _lru_in_kernel(x_ref, mod_ref, g_ref, w_ref, gate_ref, rec_ref, *, d, lw):
    tt, nb, _ = x_ref.shape
    tm = tt * nb
    mod = mod_ref[0]
    h = _normmod(x_ref[...], g_ref[...], mod[:, d:2 * d], mod[:, 0:d])
    hb = h.reshape(tm, d).astype(_BF)
    cw = 4 * LANES
    for c0 in range(0, lw, cw):
        acc = _dot(hb, w_ref[:, c0:c0 + cw])
        gate_ref[:, :, c0:c0 + cw] = jax.nn.gelu(acc).reshape(tt, nb, cw)
    for c0 in range(0, lw, cw):
        acc = _dot(hb, w_ref[:, lw + c0:lw + c0 + cw])
        rec_ref[:, :, c0:c0 + cw] = acc.reshape(tt, nb, cw)


def _lru_in_call(x3, modtab, gain, w, *, tt, ctx_len):
    ta, nb, d = x3.shape
    lw = w.shape[1] // 2
    nctx = ctx_len // tt
    kern = functools.partial(_lru_in_kernel, d=d, lw=lw)
    return pl.pallas_call(
        kern,
        out_shape=(jax.ShapeDtypeStruct((ta, nb, lw), _F32),
                   jax.ShapeDtypeStruct((ta, nb, lw), _F32)),
        grid=(ta // tt,),
        in_specs=[pl.BlockSpec((tt, nb, d), lambda i: (i, 0, 0)),
                  pl.BlockSpec((1, nb, 6 * d), lambda i: (jnp.where(i < nctx, 0, 1), 0, 0)),
                  pl.BlockSpec((1, d), lambda i: (0, 0)),
                  pl.BlockSpec((d, 2 * lw), lambda i: (0, 0))],
        out_specs=(pl.BlockSpec((tt, nb, lw), lambda i: (i, 0, 0)),
                   pl.BlockSpec((tt, nb, lw), lambda i: (i, 0, 0))),
        compiler_params=_cparams(1),
        name="lru_in",
    )(x3, modtab, gain, w)


def _attn_kernel(q_ref, k_ref, v_ref, lam_ref, sg_ref, o_ref, *, n_ctx_tiles, ctx_len,
                 lambda_init):
    tq, hw = q_ref.shape
    hd = hw // 2
    q = q_ref[...]
    lane = lax.broadcasted_iota(jnp.int32, (tq, hw), 1)
    zero = jnp.zeros_like(q)
    q2 = jnp.concatenate([jnp.where(lane < hd, q, zero), jnp.where(lane < hd, zero, q)], axis=0)
    lv = lam_ref[...]
    lam = (jnp.exp(jnp.sum(lv[0:1] * lv[1:2], axis=-1, keepdims=True))
           - jnp.exp(jnp.sum(lv[2:3] * lv[3:4], axis=-1, keepdims=True)) + lambda_init)

    def attend(nk):
        k = k_ref[0:nk, :]
        v = v_ref[0:nk, :]
        s = lax.dot_general(q2, k, (((1,), (1,)), ((), ())), preferred_element_type=_F32)
        m = jnp.max(s, axis=-1, keepdims=True)
        e = jnp.exp(s - m)
        r = 1.0 / jnp.sum(e, axis=-1, keepdims=True)
        w = e[0:tq] * r[0:tq] - e[tq:2 * tq] * (lam * r[tq:2 * tq])
        o = _dot(w.astype(_BF), v)
        o = _rms(o) * sg_ref[...] * (1.0 - lambda_init)
        o_ref[...] = o.astype(o_ref.dtype)

    qi = pl.program_id(2)

    @pl.when(qi < n_ctx_tiles)
    def _():
        attend(ctx_len)

    @pl.when(qi >= n_ctx_tiles)
    def _():
        attend(k_ref.shape[0])


def _attn_call(qkv2, lam_vecs, subln, *, nb, d, ctx_len, tq, lambda_init):
    ta = qkv2.shape[0]
    hw = d // ATTN_HEADS
    nblk = 3 * ATTN_HEADS
    kern = functools.partial(_attn_kernel, n_ctx_tiles=ctx_len // tq, ctx_len=ctx_len,
                             lambda_init=lambda_init)
    return pl.pallas_call(
        kern,
        out_shape=jax.ShapeDtypeStruct((ta, nb * d), _BF),
        grid=(nb, ATTN_HEADS, ta // tq),
        in_specs=[pl.BlockSpec((tq, hw), lambda b, h, i: (i, b * nblk + h)),
                  pl.BlockSpec((ta, hw), lambda b, h, i: (0, b * nblk + ATTN_HEADS + h)),
                  pl.BlockSpec((ta, hw), lambda b, h, i: (0, b * nblk + 2 * ATTN_HEADS + h)),
                  pl.BlockSpec(lam_vecs.shape, lambda b, h, i: (0, 0)),
                  pl.BlockSpec((1, hw), lambda b, h, i: (0, 0))],
        out_specs=pl.BlockSpec((tq, hw), lambda b, h, i: (i, b * ATTN_HEADS + h)),
        compiler_params=_cparams(3),
        name="diff_attn",
    )(qkv2, qkv2, qkv2, lam_vecs, subln)


def _mlp_kernel(x_ref, a_ref, wa_ref, mod_ref, ng_ref, w1_ref, w2_ref, o_ref,
                x1_ref, h2_ref, acc_ref, *, d, glu):
    tt, nb, _ = x_ref.shape
    tm = tt * nb
    f = pl.program_id(1)

    @pl.when(f == 0)
    def _():
        mod = mod_ref[0]
        y = _dot(a_ref[...], wa_ref[...])
        if glu:
            y = y[:, 0:d] * jax.nn.sigmoid(y[:, d:2 * d])
        y3 = y.reshape(tt, nb, d)
        x1 = x_ref[...] + mod[:, 2 * d:3 * d] * (_rms(y3) * ng_ref[1:2])
        x1_ref[...] = x1
        h2 = _normmod(x1, ng_ref[2:3], mod[:, 4 * d:5 * d], mod[:, 3 * d:4 * d])
        h2_ref[...] = h2.reshape(tm, d).astype(_BF)
        acc_ref[...] = jnp.zeros_like(acc_ref)

    t = _dot(h2_ref[...], w1_ref[...])
    t = jnp.maximum(t, 0.0)
    t = (t * t).astype(_BF)
    acc_ref[...] += _dot(t, w2_ref[...])

    @pl.when(f == pl.num_programs(1) - 1)
    def _():
        mod = mod_ref[0]
        m3 = acc_ref[...].reshape(tt, nb, d)
        o_ref[...] = x1_ref[...] + mod[:, 5 * d:6 * d] * (_rms(m3) * ng_ref[3:4])


def _mlp_call(x3, a2, wa, modtab, ng, w1, w2, *, tt, tf, ctx_len, glu):
    ta, nb, d = x3.shape
    tm = tt * nb
    ka, na = wa.shape
    dff = w1.shape[1]
    nctx = ctx_len // tt
    kern = functools.partial(_mlp_kernel, d=d, glu=glu)
    return pl.pallas_call(
        kern,
        out_shape=jax.ShapeDtypeStruct((ta, nb, d), _F32),
        grid=(ta // tt, dff // tf),
        in_specs=[pl.BlockSpec((tt, nb, d), lambda i, f: (i, 0, 0)),
                  pl.BlockSpec((tm, ka), lambda i, f: (i, 0)),
                  pl.BlockSpec((ka, na), lambda i, f: (0, 0)),
                  pl.BlockSpec((1, nb, 6 * d), lambda i, f: (jnp.where(i < nctx, 0, 1), 0, 0)),
                  pl.BlockSpec((4, d), lambda i, f: (0, 0)),
                  pl.BlockSpec((d, tf), lambda i, f: (0, f)),
                  pl.BlockSpec((tf, d), lambda i, f: (f, 0))],
        out_specs=pl.BlockSpec((tt, nb, d), lambda i, f: (i, 0, 0)),
        scratch_shapes=[pltpu.VMEM((tt, nb, d), _F32),
                        pltpu.VMEM((tm, d), _BF),
                        pltpu.VMEM((tm, d), _F32)],
        compiler_params=_cparams(2),
        name="proj_mlp",
    )(x3, a2, wa, modtab, ng, w1, w2)


def _chunk_of(i, n_chunks, n_ctx_chunks, reverse):
    if not reverse:
        return i
    return jnp.where(i < n_ctx_chunks, n_ctx_chunks - 1 - i, n_chunks - 1 - (i - n_ctx_chunks))


def _s5_kernel(*refs, d, reverse, final):
    if final:
        (x_ref, mod_ref, g_ref, bw_ref, cw_ref, lam_ref, yb_ref, dsk_ref,
         o_ref, bu_ref, h_ref) = refs
    else:
        x_ref, mod_ref, g_ref, bw_ref, cw_ref, lam_ref, o_ref, bu_ref, h_ref = refs
    tt, nb, _ = x_ref.shape
    tm = tt * nb
    nslab = d // LANES
    sw = bw_ref.shape[2]
    hw = sw // 2

    @pl.when(pl.program_id(0) == 0)
    def _():
        h_ref[...] = jnp.zeros_like(h_ref)

    mod = mod_ref[0]
    u3 = _normmod(x_ref[...], g_ref[...], mod[:, d:2 * d], mod[:, 0:d])
    ub = u3.reshape(tm, d).astype(_BF)
    for j in range(nslab):
        bu = _dot(ub[:, j * LANES:(j + 1) * LANES], bw_ref[j])
        bu_ref[:, :, j * sw:(j + 1) * sw] = bu.reshape(tt, nb, sw)

    for j in range(nslab):
        cr = j * sw
        ci = cr + hw
        lr = lam_ref[j]
        li = lam_ref[nslab + j]

        def body(s, carry, cr=cr, ci=ci, lr=lr, li=li):
            hr, hi = carry
            idx = tt - 1 - s if reverse else s
            nr = lr * hr - li * hi + bu_ref[idx, :, cr:cr + hw]
            ni = lr * hi + li * hr + bu_ref[idx, :, ci:ci + hw]
            bu_ref[idx, :, cr:cr + hw] = nr
            bu_ref[idx, :, ci:ci + hw] = ni
            return nr, ni

        hr, hi = lax.fori_loop(0, tt, body, (h_ref[:, cr:cr + hw], h_ref[:, ci:ci + hw]),
                               unroll=4)
        h_ref[:, cr:cr + hw] = hr
        h_ref[:, ci:ci + hw] = hi

    for j in range(nslab):
        hs = bu_ref[:, :, j * sw:(j + 1) * sw].reshape(tm, sw).astype(_BF)
        y = _dot(hs, cw_ref[j]).reshape(tt, nb, LANES)
        cols = slice(j * LANES, (j + 1) * LANES)
        if final:
            tot = y + yb_ref[:, :, cols] + dsk_ref[:, cols] * u3[:, :, cols]
            o_ref[:, cols] = jax.nn.gelu(tot).reshape(tm, LANES).astype(o_ref.dtype)
        else:
            o_ref[:, :, cols] = y


def _s5_call(x3, modtab, gain, bw, cw, lam, yb, dskip, *, tt, ctx_len, reverse):
    ta, nb, d = x3.shape
    tm = tt * nb
    n_chunks = ta // tt
    nctx = ctx_len // tt
    final = yb is not None
    nslab, _, sw = bw.shape

    def cidx(i):
        return _chunk_of(i, n_chunks, nctx, reverse)

    in_specs = [pl.BlockSpec((tt, nb, d), lambda i: (cidx(i), 0, 0)),
                pl.BlockSpec((1, nb, 6 * d), lambda i: (jnp.where(cidx(i) < nctx, 0, 1), 0, 0)),
                pl.BlockSpec((1, d), lambda i: (0, 0)),
                pl.BlockSpec(bw.shape, lambda i: (0, 0, 0)),
                pl.BlockSpec(cw.shape, lambda i: (0, 0, 0)),
                pl.BlockSpec(lam.shape, lambda i: (0, 0, 0))]
    args = [x3, modtab, gain, bw, cw, lam]
    if final:
        in_specs += [pl.BlockSpec((tt, nb, d), lambda i: (cidx(i), 0, 0)),
                     pl.BlockSpec((1, d), lambda i: (0, 0))]
        args += [yb, dskip]
        out_shape = jax.ShapeDtypeStruct((ta * nb, d), _BF)
        out_spec = pl.BlockSpec((tm, d), lambda i: (cidx(i), 0))
    else:
        out_shape = jax.ShapeDtypeStruct((ta, nb, d), _F32)
        out_spec = pl.BlockSpec((tt, nb, d), lambda i: (cidx(i), 0, 0))
    kern = functools.partial(_s5_kernel, d=d, reverse=reverse, final=final)
    return pl.pallas_call(
        kern,
        out_shape=out_shape,
        grid=(n_chunks,),
        in_specs=in_specs,
        out_specs=out_spec,
        scratch_shapes=[pltpu.VMEM((tt, nb, nslab * sw), _F32),
                        pltpu.VMEM((nb, nslab * sw), _F32)],
        compiler_params=_cparams(1),
        name="s5_final" if final else "s5_first",
    )(*args)


def _s5_weights(a_re, a_im, b_re, b_im, c_re, c_im, log_dt, nb):
    ng, p = a_re.shape
    gc = b_re.shape[-1]
    gps = LANES // gc
    nslab = ng // gps
    lam = lax.complex(a_re.astype(_F32), a_im.astype(_F32))
    dt = jnp.exp(log_dt.astype(_F32))[:, None]
    lam_bar = jnp.exp(lam * dt)
    b_mat = lax.complex(b_re.astype(_F32), b_im.astype(_F32))
    b_bar = ((lam_bar - 1.0) / lam)[..., None] * b_mat
    eye = jnp.eye(gps, dtype=_F32)

    def b_block(part):
        blk = part.reshape(nslab, gps, p, gc)
        return jnp.einsum('jgpc,gh->jgchp', blk, eye).reshape(nslab, gps * gc, gps * p)

    bw = jnp.concatenate([b_block(jnp.real(b_bar)), b_block(jnp.imag(b_bar))], axis=-1)

    def c_block(part):
        blk = part.reshape(nslab, gps, gc, p)
        return jnp.einsum('jgcp,gh->jgphc', blk, eye).reshape(nslab, gps * p, gps * gc)

    cw = jnp.concatenate([c_block(c_re.astype(_F32)), c_block(-c_im.astype(_F32))], axis=1)

    def lam_rows(part):
        return jnp.broadcast_to(part.reshape(nslab, 1, gps * p), (nslab, nb, gps * p))

    lam_t = jnp.concatenate([lam_rows(jnp.real(lam_bar)), lam_rows(jnp.imag(lam_bar))], axis=0)
    return bw.astype(_BF), cw.astype(_BF), lam_t


def _lru_kernel(*refs, tt, n_chunks, n_ctx_chunks, reverse, final):
    if final:
        (rec_ref, prev_ref, next_ref, cw_ref, cb_ref, wg_ref, bg_ref, ap_ref, hb_ref, gate_ref,
         o_ref, a_ref, b_ref, h_ref) = refs
    else:
        (rec_ref, prev_ref, next_ref, cw_ref, cb_ref, wg_ref, bg_ref, ap_ref,
         o_ref, a_ref, b_ref, h_ref) = refs
    _, nb, lw = rec_ref.shape
    tm = tt * nb
    nblk, bwid, _ = wg_ref.shape
    i = pl.program_id(0)

    @pl.when(i == 0)
    def _():
        h_ref[...] = jnp.zeros_like(h_ref)

    c = _chunk_of(i, n_chunks, n_ctx_chunks, reverse)
    seg_start = jnp.logical_or(c == 0, c == n_ctx_chunks)
    seg_end = jnp.logical_or(c == n_ctx_chunks - 1, c == n_chunks - 1)
    prev = jnp.where(seg_start, 0.0, prev_ref[...])
    nxt = jnp.where(seg_end, 0.0, next_ref[...])
    ext = jnp.concatenate([prev, rec_ref[...], nxt], axis=0)
    u3 = cb_ref[...]
    for k in range(CONV_W):
        u3 = u3 + cw_ref[k:k + 1] * ext[k:k + tt]
    u = u3.reshape(tm, lw)
    ub = u.astype(_BF)
    sp = jax.nn.softplus(-ap_ref[...])
    for n in range(nblk):
        cols = slice(n * bwid, (n + 1) * bwid)
        g = _dot(ub[:, cols], wg_ref[n])
        r = jax.nn.sigmoid(g[:, 0:bwid] + bg_ref[0:1, cols])
        ig = jax.nn.sigmoid(g[:, bwid:2 * bwid] + bg_ref[1:2, cols])
        log_a = -LRU_C * r * sp[:, cols]
        a = jnp.exp(log_a)
        a_ref[:, :, cols] = a.reshape(tt, nb, bwid)
        bterm = jnp.sqrt(-jnp.tanh(log_a) * (1.0 + a * a)) * (ig * u[:, cols])
        b_ref[:, :, cols] = bterm.reshape(tt, nb, bwid)

    sc = 4 * LANES
    for c0 in range(0, lw, sc):

        def body(s, h, c0=c0):
            idx = tt - 1 - s if reverse else s
            h = a_ref[idx, :, c0:c0 + sc] * h + b_ref[idx, :, c0:c0 + sc]
            b_ref[idx, :, c0:c0 + sc] = h
            return h

        h_ref[:, c0:c0 + sc] = lax.fori_loop(0, tt, body, h_ref[:, c0:c0 + sc], unroll=4)

    if final:
        tot = (b_ref[...] + hb_ref[...]) * gate_ref[...]
        o_ref[...] = tot.reshape(tm, lw).astype(o_ref.dtype)
    else:
        o_ref[...] = b_ref[...]


def _lru_call(rec3, conv_w, conv_b, wg, bg, ap, hb, gate3, *, tt, ctx_len, reverse):
    ta, nb, lw = rec3.shape
    tm = tt * nb
    n_chunks = ta // tt
    nctx = ctx_len // tt
    final = hb is not None
    half = tt // 2

    def cidx(i):
        return _chunk_of(i, n_chunks, nctx, reverse)

    in_specs = [pl.BlockSpec((tt, nb, lw), lambda i: (cidx(i), 0, 0)),
                pl.BlockSpec((2, nb, lw), lambda i: (jnp.maximum(cidx(i) * half - 1, 0), 0, 0)),
                pl.BlockSpec((1, nb, lw), lambda i: (jnp.minimum((cidx(i) + 1) * tt, ta - 1), 0, 0)),
                pl.BlockSpec(conv_w.shape, lambda i: (0, 0)),
                pl.BlockSpec(conv_b.shape, lambda i: (0, 0)),
                pl.BlockSpec(wg.shape, lambda i: (0, 0, 0)),
                pl.BlockSpec(bg.shape, lambda i: (0, 0)),
                pl.BlockSpec(ap.shape, lambda i: (0, 0))]
    args = [rec3, rec3, rec3, conv_w, conv_b, wg, bg, ap]
    if final:
        in_specs += [pl.BlockSpec((tt, nb, lw), lambda i: (cidx(i), 0, 0)),
                     pl.BlockSpec((tt, nb, lw), lambda i: (cidx(i), 0, 0))]
        args += [hb, gate3]
        out_shape = jax.ShapeDtypeStruct((ta * nb, lw), _BF)
        out_spec = pl.BlockSpec((tm, lw), lambda i: (cidx(i), 0))
    else:
        out_shape = jax.ShapeDtypeStruct((ta, nb, lw), _F32)
        out_spec = pl.BlockSpec((tt, nb, lw), lambda i: (cidx(i), 0, 0))
    kern = functools.partial(_lru_kernel, tt=tt, n_chunks=n_chunks, n_ctx_chunks=nctx,
                             reverse=reverse, final=final)
    return pl.pallas_call(
        kern,
        out_shape=out_shape,
        grid=(n_chunks,),
        in_specs=in_specs,
        out_specs=out_spec,
        scratch_shapes=[pltpu.VMEM((tt, nb, lw), _F32),
                        pltpu.VMEM((tt, nb, lw), _F32),
                        pltpu.VMEM((nb, lw), _F32)],
        compiler_params=_cparams(1),
        name="lru_final" if final else "lru_first",
    )(*args)


def _rope_tables(seq, ctx_len, nb, hd):
    n_freq = hd // 4
    pos = jnp.arange(seq)
    row = (pos // GRID_W).astype(_F32)
    col = (pos % GRID_W).astype(_F32)
    inv = ROPE_BASE ** (-jnp.arange(n_freq, dtype=_F32) / n_freq)
    ang = jnp.concatenate([row[:, None] * inv, col[:, None] * inv], axis=-1)
    cos, sin = jnp.cos(ang), jnp.sin(ang)
    cos_c = jnp.tile(jnp.concatenate([cos, cos], axis=-1), (1, LANES // hd))
    sin_c = jnp.tile(jnp.concatenate([-sin, sin], axis=-1), (1, LANES // hd))
    cos_c = jnp.concatenate([jnp.ones((ctx_len, LANES), _F32), cos_c], axis=0)
    sin_c = jnp.concatenate([jnp.zeros((ctx_len, LANES), _F32), sin_c], axis=0)
    return jnp.repeat(cos_c, nb, axis=0), jnp.repeat(sin_c, nb, axis=0)


def kernel(x, c, ctx, c_ctx, ada_w, ada_b, norm_g, mlp_w1, mlp_w2, attn_w_qkv, attn_w_o, attn_lambda, attn_subln, s5_a_re, s5_a_im, s5_b_re, s5_b_im, s5_c_re, s5_c_im, s5_log_dt, s5_d, s5_w_glu, lru_w_in, lru_conv_w, lru_conv_b, lru_w_gate, lru_b_gate, lru_a_param, lru_w_out):
    nb, seq, d = x.shape
    ctx_len = ctx.shape[1]
    depth = ada_w.shape[0]
    assert nb % SUBLANES == 0 and d % LANES == 0
    tt = 32
    tq = 256
    tf = 512
    assert ctx_len % tq == 0 and seq % tq == 0 and ctx_len % tt == 0 and seq % tt == 0

    x3 = jnp.transpose(jnp.concatenate([ctx, x], axis=1), (1, 0, 2))

    pad = (-(nb + 1)) % SUBLANES
    c_all = jnp.concatenate([c, c_ctx[None, :], jnp.zeros((pad, d), c.dtype)], axis=0)
    mod_all = _ada_call(c_all, ada_w.astype(_BF), ada_b)
    mod_lat = mod_all[:, :nb]
    mod_ctx = jnp.broadcast_to(mod_all[:, nb:nb + 1], mod_lat.shape)
    modtabs = jnp.stack([mod_ctx, mod_lat], axis=1)

    cos_t, sin_t = _rope_tables(seq, ctx_len, nb, d // ATTN_HEADS // 2)
    w1b = mlp_w1.astype(_BF)
    w2b = mlp_w2.astype(_BF)

    for i in range(depth):
        kind, j = i % N_MIXERS, i // N_MIXERS
        modtab = modtabs[i]
        gain0 = norm_g[i, 0:1]
        if kind == 0:
            lambda_init = 0.8 - 0.6 * math.exp(-0.3 * i)
            qkv = _qkv_call(x3, modtab, gain0, attn_w_qkv[j].astype(_BF), cos_t, sin_t,
                            tt=tt, ctx_len=ctx_len)
            qkv2 = qkv.reshape(ctx_len + seq, nb * 3 * d)
            o2 = _attn_call(qkv2, attn_lambda[j], attn_subln[j][None, :], nb=nb, d=d,
                            ctx_len=ctx_len, tq=tq, lambda_init=lambda_init)
            a2 = o2.reshape((ctx_len + seq) * nb, d)
            wa = attn_w_o[j].astype(_BF)
            glu = False
        elif kind == 1:
            wts = [_s5_weights(s5_a_re[j, dr], s5_a_im[j, dr], s5_b_re[j, dr], s5_b_im[j, dr],
                               s5_c_re[j, dr], s5_c_im[j, dr], s5_log_dt[j, dr], nb)
                   for dr in range(2)]
            yb = _s5_call(x3, modtab, gain0, *wts[1], None, None, tt=tt, ctx_len=ctx_len,
                          reverse=True)
            a2 = _s5_call(x3, modtab, gain0, *wts[0], yb, s5_d[j][None, :], tt=tt,
                          ctx_len=ctx_len, reverse=False)
            wa = s5_w_glu[j].astype(_BF)
            glu = True
        else:
            gate3, rec3 = _lru_in_call(x3, modtab, gain0, lru_w_in[j].astype(_BF), tt=tt,
                                       ctx_len=ctx_len)
            wg = [jnp.concatenate([lru_w_gate[j, dr, 0], lru_w_gate[j, dr, 1]], axis=-1).astype(_BF)
                  for dr in range(2)]
            cb = lru_conv_b[j][None, :]
            hb = _lru_call(rec3, lru_conv_w[j], cb, wg[1], lru_b_gate[j, 1],
                           lru_a_param[j, 1][None, :], None, None, tt=tt, ctx_len=ctx_len,
                           reverse=True)
            a2 = _lru_call(rec3, lru_conv_w[j], cb, wg[0], lru_b_gate[j, 0],
                           lru_a_param[j, 0][None, :], hb, gate3, tt=tt, ctx_len=ctx_len,
                           reverse=False)
            wa = lru_w_out[j].astype(_BF)
            glu = False
        x3 = _mlp_call(x3, a2, wa, modtab, norm_g[i], w1b[i], w2b[i], tt=tt, tf=tf,
                       ctx_len=ctx_len, glu=glu)

    return jnp.transpose(x3[ctx_len:], (1, 0, 2))
```

```python
import functools
import math

import jax
import jax.numpy as jnp
from jax import lax
from jax.experimental import pallas as pl
from jax.experimental.pallas import tpu as pltpu

EPS = 1e-6
GRID_W = 64
ROPE_BASE = 10000.0
N_MIXERS = 3
ATTN_HEADS = 8
S5_GROUP = 16
S5_STATE = 64
LRU_BW = 256
LRU_C = 8.0
CONV_W = 4
CONV_LEFT = 2

LANES = 128
SUBLANES = 8
VMEM_LIMIT = 56 * 1024 * 1024

_BF = jnp.bfloat16
_F32 = jnp.float32


def _cparams(n_axes):
    return pltpu.CompilerParams(dimension_semantics=("arbitrary",) * n_axes,
                                vmem_limit_bytes=VMEM_LIMIT)


def _rms(x):
    return x * lax.rsqrt(jnp.mean(x * x, axis=-1, keepdims=True) + EPS)


def _normmod(x3, gain, scale, shift):
    return _rms(x3) * gain * (1.0 + scale) + shift


def _dot(a, b):
    return jnp.dot(a, b, preferred_element_type=_F32)


def _ada_kernel(c_ref, w_ref, b_ref, o_ref):
    sc = jax.nn.silu(c_ref[...]).astype(_BF)
    o_ref[0] = _dot(sc, w_ref[0]) + b_ref[0]


def _ada_call(c_all, ada_w, ada_b):
    depth, d, n = ada_w.shape
    rows = c_all.shape[0]
    tn = n // 4
    return pl.pallas_call(
        _ada_kernel,
        out_shape=jax.ShapeDtypeStruct((depth, rows, n), _F32),
        grid=(depth, n // tn),
        in_specs=[pl.BlockSpec((rows, d), lambda l, j: (0, 0)),
                  pl.BlockSpec((1, d, tn), lambda l, j: (l, 0, j)),
                  pl.BlockSpec((1, 1, tn), lambda l, j: (l, 0, j))],
        out_specs=pl.BlockSpec((1, rows, tn), lambda l, j: (l, 0, j)),
        compiler_params=_cparams(2),
        name="adaln",
    )(c_all, ada_w, ada_b.reshape(depth, 1, n))


def _rope_cols(t, cos, sin, lo_half):
    rot = jnp.where(lo_half, pltpu.roll(t, 96, 1), pltpu.roll(t, 32, 1))
    return t * cos + rot * sin


def _qkv_kernel(x_ref, mod_ref, g_ref, w_ref, cos_ref, sin_ref, o_ref, *, d, q_scale, x_bm):
    nb, tt, _ = o_ref.shape
    tm = tt * nb
    x3 = pltpu.einshape("btd->tbd", x_ref[...]) if x_bm else x_ref[...]
    mod = mod_ref[0]
    h = _normmod(x3, g_ref[...], mod[:, d:2 * d], mod[:, 0:d])
    hb = pltpu.einshape("tbd->btd", h.astype(_BF)).reshape(tm, d)
    cos = jnp.broadcast_to(cos_ref[...][None], (nb, tt, LANES)).reshape(tm, LANES)
    sin = jnp.broadcast_to(sin_ref[...][None], (nb, tt, LANES)).reshape(tm, LANES)
    lane = lax.broadcasted_iota(jnp.int32, (tm, LANES), 1)
    lo_half = (lane % 64) < 32
    n_rope = 2 * d // LANES
    cw = 4 * LANES
    for c0 in range(0, 3 * d, cw):
        acc = _dot(hb, w_ref[:, c0:c0 + cw])
        for k in range(cw // LANES):
            col = c0 // LANES + k
            t = acc[:, k * LANES:(k + 1) * LANES]
            if col < n_rope:
                t = _rope_cols(t, cos, sin, lo_half)
                if col < n_rope // 2:
                    t = t * q_scale
            o_ref[:, :, col * LANES:(col + 1) * LANES] = t.reshape(nb, tt, LANES).astype(o_ref.dtype)


def _qkv_call(x3, modtab, gain, w, cos_t, sin_t, *, tt, ctx_len, x_bm):
    if x_bm:
        nb, ta, d = x3.shape
        x_spec = pl.BlockSpec((nb, tt, d), lambda i: (0, i, 0))
    else:
        ta, nb, d = x3.shape
        x_spec = pl.BlockSpec((tt, nb, d), lambda i: (i, 0, 0))
    n = w.shape[1]
    nctx = ctx_len // tt
    kern = functools.partial(_qkv_kernel, d=d, q_scale=(d // ATTN_HEADS // 2) ** -0.5, x_bm=x_bm)
    return pl.pallas_call(
        kern,
        out_shape=jax.ShapeDtypeStruct((nb, ta, n), _BF),
        grid=(ta // tt,),
        in_specs=[x_spec,
                  pl.BlockSpec((1, nb, 6 * d), lambda i: (jnp.where(i < nctx, 0, 1), 0, 0)),
                  pl.BlockSpec((1, d), lambda i: (0, 0)),
                  pl.BlockSpec((d, n), lambda i: (0, 0)),
                  pl.BlockSpec((tt, LANES), lambda i: (i, 0)),
                  pl.BlockSpec((tt, LANES), lambda i: (i, 0))],
        out_specs=pl.BlockSpec((nb, tt, n), lambda i: (0, i, 0)),
        compiler_params=_cparams(1),
        name="qkv_rope",
    )(x3, modtab, gain, w, cos_t, sin_t)


def _lru_in_kernel(x_ref, mod_ref, g_ref, w_ref, gate_ref, rec_ref, *, d, lw):
    tt, nb, _ = x_ref.shape
    tm = tt * nb
    mod = mod_ref[0]
    h = _normmod(x_ref[...], g_ref[...], mod[:, d:2 * d], mod[:, 0:d])
    hb = h.reshape(tm, d).astype(_BF)
    cw = 4 * LANES
    for c0 in range(0, lw, cw):
        acc = _dot(hb, w_ref[:, c0:c0 + cw])
        gate_ref[:, :, c0:c0 + cw] = jax.nn.gelu(acc).reshape(tt, nb, cw)
    for c0 in range(0, lw, cw):
        acc = _dot(hb, w_ref[:, lw + c0:lw + c0 + cw])
        rec_ref[:, :, c0:c0 + cw] = acc.reshape(tt, nb, cw)


def _lru_in_call(x3, modtab, gain, w, *, tt, ctx_len):
    ta, nb, d = x3.shape
    lw = w.shape[1] // 2
    nctx = ctx_len // tt
    kern = functools.partial(_lru_in_kernel, d=d, lw=lw)
    return pl.pallas_call(
        kern,
        out_shape=(jax.ShapeDtypeStruct((ta, nb, lw), _F32),
                   jax.ShapeDtypeStruct((ta, nb, lw), _F32)),
        grid=(ta // tt,),
        in_specs=[pl.BlockSpec((tt, nb, d), lambda i: (i, 0, 0)),
                  pl.BlockSpec((1, nb, 6 * d), lambda i: (jnp.where(i < nctx, 0, 1), 0, 0)),
                  pl.BlockSpec((1, d), lambda i: (0, 0)),
                  pl.BlockSpec((d, 2 * lw), lambda i: (0, 0))],
        out_specs=(pl.BlockSpec((tt, nb, lw), lambda i: (i, 0, 0)),
                   pl.BlockSpec((tt, nb, lw), lambda i: (i, 0, 0))),
        compiler_params=_cparams(1),
        name="lru_in",
    )(x3, modtab, gain, w)


def _attn_kernel(q_ref, k_ref, v_ref, lam_ref, sg_ref, o_ref, sa_ref, sb_ref, *, tq, ctx_len,
                 lambda_init):
    ta, hw = k_ref.shape
    hd = hw // 2
    n_tiles = ta // tq
    n_ctx = ctx_len // tq
    lane = lax.broadcasted_iota(jnp.int32, (tq, hw), 1)
    lv = lam_ref[...]
    lam = (jnp.exp(jnp.sum(lv[0:1] * lv[1:2], axis=-1, keepdims=True))
           - jnp.exp(jnp.sum(lv[2:3] * lv[3:4], axis=-1, keepdims=True)) + lambda_init)

    def scores(r0, s_ref, nk):
        q = q_ref[pl.ds(r0, tq), :]
        zero = jnp.zeros_like(q)
        q2 = jnp.concatenate([jnp.where(lane < hd, q, zero), jnp.where(lane < hd, zero, q)], axis=0)
        s_ref[:, 0:nk] = lax.dot_general(q2, k_ref[0:nk, :], (((1,), (1,)), ((), ())),
                                         preferred_element_type=_F32)

    def softmax_pv(s_ref, r0, nk):
        s = s_ref[:, 0:nk]
        m = jnp.max(s, axis=-1, keepdims=True)
        e = jnp.exp(s - m)
        r = 1.0 / jnp.sum(e, axis=-1, keepdims=True)
        w = e[0:tq] * r[0:tq] - e[tq:2 * tq] * (lam * r[tq:2 * tq])
        o = _dot(w.astype(_BF), v_ref[0:nk, :])
        o = _rms(o) * sg_ref[...] * (1.0 - lambda_init)
        o_ref[pl.ds(r0, tq), :] = o.astype(o_ref.dtype)

    for c in range(n_ctx):
        scores(c * tq, sa_ref, ctx_len)
        softmax_pv(sa_ref, c * tq, ctx_len)

    n_lat = n_tiles - n_ctx
    scores(n_ctx * tq, sa_ref, ta)

    def pair(p, carry):
        i = n_ctx + 2 * p
        r0 = pl.multiple_of(i * tq, tq)
        r1 = pl.multiple_of((i + 1) * tq, tq)
        r2 = pl.multiple_of(jnp.minimum(i + 2, n_tiles - 1) * tq, tq)
        scores(r1, sb_ref, ta)
        softmax_pv(sa_ref, r0, ta)
        scores(r2, sa_ref, ta)
        softmax_pv(sb_ref, r1, ta)
        return carry

    lax.fori_loop(0, n_lat // 2, pair, 0)


def _attn_call(qkv, lam_vecs, subln, *, ctx_len, tq, lambda_init):
    nb, ta, n = qkv.shape
    d = n // 3
    hw = d // ATTN_HEADS
    assert ((ta - ctx_len) // tq) % 2 == 0
    kern = functools.partial(_attn_kernel, tq=tq, ctx_len=ctx_len, lambda_init=lambda_init)
    return pl.pallas_call(
        kern,
        out_shape=jax.ShapeDtypeStruct((nb, ta, d), _BF),
        grid=(nb, ATTN_HEADS),
        in_specs=[pl.BlockSpec((None, ta, hw), lambda b, h: (b, 0, h)),
                  pl.BlockSpec((None, ta, hw), lambda b, h: (b, 0, ATTN_HEADS + h)),
                  pl.BlockSpec((None, ta, hw), lambda b, h: (b, 0, 2 * ATTN_HEADS + h)),
                  pl.BlockSpec(lam_vecs.shape, lambda b, h: (0, 0)),
                  pl.BlockSpec((1, hw), lambda b, h: (0, 0))],
        out_specs=pl.BlockSpec((None, ta, hw), lambda b, h: (b, 0, h)),
        scratch_shapes=[pltpu.VMEM((2 * tq, ta), _F32), pltpu.VMEM((2 * tq, ta), _F32)],
        compiler_params=_cparams(2),
        name="diff_attn",
    )(qkv, qkv, qkv, lam_vecs, subln)


def _mlp_kernel(x_ref, a_ref, wa_ref, mod_ref, ng_ref, w1_ref, w2_ref, o_ref,
                x1_ref, h2_ref, acc_ref, *, d, glu, x_bm, a_bm, out_bm):
    tt, nb, _ = x1_ref.shape
    tm = tt * nb
    f = pl.program_id(1)

    @pl.when(f == 0)
    def _():
        mod = mod_ref[0]
        a = a_ref[...]
        if a_bm:
            a = a.reshape(tm, a.shape[-1])
        y = _dot(a, wa_ref[...])
        if glu:
            y = y[:, 0:d] * jax.nn.sigmoid(y[:, d:2 * d])
        y3 = pltpu.einshape("btd->tbd", y.reshape(nb, tt, d)) if a_bm else y.reshape(tt, nb, d)
        x3 = pltpu.einshape("btd->tbd", x_ref[...]) if x_bm else x_ref[...]
        x1 = x3 + mod[:, 2 * d:3 * d] * (_rms(y3) * ng_ref[1:2])
        x1_ref[...] = x1
        h2 = _normmod(x1, ng_ref[2:3], mod[:, 4 * d:5 * d], mod[:, 3 * d:4 * d])
        h2_ref[...] = h2.reshape(tm, d).astype(_BF)
        acc_ref[...] = jnp.zeros_like(acc_ref)

    t = _dot(h2_ref[...], w1_ref[...])
    t = jnp.maximum(t, 0.0)
    t = (t * t).astype(_BF)
    acc_ref[...] += _dot(t, w2_ref[...])

    @pl.when(f == pl.num_programs(1) - 1)
    def _():
        mod = mod_ref[0]
        m3 = acc_ref[...].reshape(tt, nb, d)
        out = x1_ref[...] + mod[:, 5 * d:6 * d] * (_rms(m3) * ng_ref[3:4])
        o_ref[...] = pltpu.einshape("tbd->btd", out) if out_bm else out


def _mlp_call(x, a, wa, modtab, ng, w1, w2, *, tt, tf, ctx_len, glu, x_bm, a_bm, out_bm, skip_ctx):
    if x_bm:
        nb, ta, d = x.shape
    else:
        ta, nb, d = x.shape
    tm = tt * nb
    ka, na = wa.shape
    dff = w1.shape[1]
    nctx = ctx_len // tt
    t_off = nctx if skip_ctx else 0
    n_t = ta // tt - t_off
    x_spec = (pl.BlockSpec((nb, tt, d), lambda i, f: (0, i + t_off, 0)) if x_bm
              else pl.BlockSpec((tt, nb, d), lambda i, f: (i + t_off, 0, 0)))
    a_spec = (pl.BlockSpec((nb, tt, ka), lambda i, f: (0, i + t_off, 0)) if a_bm
              else pl.BlockSpec((tm, ka), lambda i, f: (i + t_off, 0)))
    if out_bm:
        out_shape = jax.ShapeDtypeStruct((nb, n_t * tt, d), _F32)
        out_spec = pl.BlockSpec((nb, tt, d), lambda i, f: (0, i, 0))
    else:
        out_shape = jax.ShapeDtypeStruct((n_t * tt, nb, d), _F32)
        out_spec = pl.BlockSpec((tt, nb, d), lambda i, f: (i, 0, 0))
    kern = functools.partial(_mlp_kernel, d=d, glu=glu, x_bm=x_bm, a_bm=a_bm, out_bm=out_bm)
    return pl.pallas_call(
        kern,
        out_shape=out_shape,
        grid=(n_t, dff // tf),
        in_specs=[x_spec,
                  a_spec,
                  pl.BlockSpec((ka, na), lambda i, f: (0, 0)),
                  pl.BlockSpec((1, nb, 6 * d), lambda i, f: (jnp.where(i + t_off < nctx, 0, 1), 0, 0)),
                  pl.BlockSpec((4, d), lambda i, f: (0, 0)),
                  pl.BlockSpec((d, tf), lambda i, f: (0, f)),
                  pl.BlockSpec((tf, d), lambda i, f: (f, 0))],
        out_specs=out_spec,
        scratch_shapes=[pltpu.VMEM((tt, nb, d), _F32),
                        pltpu.VMEM((tm, d), _BF),
                        pltpu.VMEM((tm, d), _F32)],
        compiler_params=_cparams(2),
        name="proj_mlp",
    )(x, a, wa, modtab, ng, w1, w2)


def _chunk_of(i, n_chunks, n_ctx_chunks, reverse):
    if not reverse:
        return i
    return jnp.where(i < n_ctx_chunks, n_ctx_chunks - 1 - i, n_chunks - 1 - (i - n_ctx_chunks))


def _s5_kernel(*refs, d, reverse, final):
    if final:
        (x_ref, mod_ref, g_ref, bw_ref, cw_ref, lam_ref, yb_ref, dsk_ref,
         o_ref, bu_ref, h_ref) = refs
    else:
        x_ref, mod_ref, g_ref, bw_ref, cw_ref, lam_ref, o_ref, bu_ref, h_ref = refs
    tt, nb, _ = x_ref.shape
    tm = tt * nb
    nslab = d // LANES
    sw = bw_ref.shape[2]
    hw = sw // 2

    @pl.when(pl.program_id(0) == 0)
    def _():
        h_ref[...] = jnp.zeros_like(h_ref)

    mod = mod_ref[0]
    u3 = _normmod(x_ref[...], g_ref[...], mod[:, d:2 * d], mod[:, 0:d])
    ub = u3.reshape(tm, d).astype(_BF)
    for j in range(nslab):
        bu = _dot(ub[:, j * LANES:(j + 1) * LANES], bw_ref[j])
        bu_ref[:, :, j * sw:(j + 1) * sw] = bu.reshape(tt, nb, sw)

    for j in range(nslab):
        cr = j * sw
        ci = cr + hw
        lr = lam_ref[j]
        li = lam_ref[nslab + j]

        def body(s, carry, cr=cr, ci=ci, lr=lr, li=li):
            hr, hi = carry
            idx = tt - 1 - s if reverse else s
            nr = lr * hr - li * hi + bu_ref[idx, :, cr:cr + hw]
            ni = lr * hi + li * hr + bu_ref[idx, :, ci:ci + hw]
            bu_ref[idx, :, cr:cr + hw] = nr
            bu_ref[idx, :, ci:ci + hw] = ni
            return nr, ni

        hr, hi = lax.fori_loop(0, tt, body, (h_ref[:, cr:cr + hw], h_ref[:, ci:ci + hw]),
                               unroll=4)
        h_ref[:, cr:cr + hw] = hr
        h_ref[:, ci:ci + hw] = hi

    for j in range(nslab):
        hs = bu_ref[:, :, j * sw:(j + 1) * sw].reshape(tm, sw).astype(_BF)
        y = _dot(hs, cw_ref[j]).reshape(tt, nb, LANES)
        cols = slice(j * LANES, (j + 1) * LANES)
        if final:
            tot = y + yb_ref[:, :, cols] + dsk_ref[:, cols] * u3[:, :, cols]
            o_ref[:, cols] = jax.nn.gelu(tot).reshape(tm, LANES).astype(o_ref.dtype)
        else:
            o_ref[:, :, cols] = y


def _s5_call(x3, modtab, gain, bw, cw, lam, yb, dskip, *, tt, ctx_len, reverse):
    ta, nb, d = x3.shape
    tm = tt * nb
    n_chunks = ta // tt
    nctx = ctx_len // tt
    final = yb is not None
    nslab, _, sw = bw.shape

    def cidx(i):
        return _chunk_of(i, n_chunks, nctx, reverse)

    in_specs = [pl.BlockSpec((tt, nb, d), lambda i: (cidx(i), 0, 0)),
                pl.BlockSpec((1, nb, 6 * d), lambda i: (jnp.where(cidx(i) < nctx, 0, 1), 0, 0)),
                pl.BlockSpec((1, d), lambda i: (0, 0)),
                pl.BlockSpec(bw.shape, lambda i: (0, 0, 0)),
                pl.BlockSpec(cw.shape, lambda i: (0, 0, 0)),
                pl.BlockSpec(lam.shape, lambda i: (0, 0, 0))]
    args = [x3, modtab, gain, bw, cw, lam]
    if final:
        in_specs += [pl.BlockSpec((tt, nb, d), lambda i: (cidx(i), 0, 0)),
                     pl.BlockSpec((1, d), lambda i: (0, 0))]
        args += [yb, dskip]
        out_shape = jax.ShapeDtypeStruct((ta * nb, d), _BF)
        out_spec = pl.BlockSpec((tm, d), lambda i: (cidx(i), 0))
    else:
        out_shape = jax.ShapeDtypeStruct((ta, nb, d), _F32)
        out_spec = pl.BlockSpec((tt, nb, d), lambda i: (cidx(i), 0, 0))
    kern = functools.partial(_s5_kernel, d=d, reverse=reverse, final=final)
    return pl.pallas_call(
        kern,
        out_shape=out_shape,
        grid=(n_chunks,),
        in_specs=in_specs,
        out_specs=out_spec,
        scratch_shapes=[pltpu.VMEM((tt, nb, nslab * sw), _F32),
                        pltpu.VMEM((nb, nslab * sw), _F32)],
        compiler_params=_cparams(1),
        name="s5_final" if final else "s5_first",
    )(*args)


def _s5_weights(a_re, a_im, b_re, b_im, c_re, c_im, log_dt, nb):
    ng, p = a_re.shape
    gc = b_re.shape[-1]
    gps = LANES // gc
    nslab = ng // gps
    lam = lax.complex(a_re.astype(_F32), a_im.astype(_F32))
    dt = jnp.exp(log_dt.astype(_F32))[:, None]
    lam_bar = jnp.exp(lam * dt)
    b_mat = lax.complex(b_re.astype(_F32), b_im.astype(_F32))
    b_bar = ((lam_bar - 1.0) / lam)[..., None] * b_mat
    eye = jnp.eye(gps, dtype=_F32)

    def b_block(part):
        blk = part.reshape(nslab, gps, p, gc)
        return jnp.einsum('jgpc,gh->jgchp', blk, eye).reshape(nslab, gps * gc, gps * p)

    bw = jnp.concatenate([b_block(jnp.real(b_bar)), b_block(jnp.imag(b_bar))], axis=-1)

    def c_block(part):
        blk = part.reshape(nslab, gps, gc, p)
        return jnp.einsum('jgcp,gh->jgphc', blk, eye).reshape(nslab, gps * p, gps * gc)

    cw = jnp.concatenate([c_block(c_re.astype(_F32)), c_block(-c_im.astype(_F32))], axis=1)

    def lam_rows(part):
        return jnp.broadcast_to(part.reshape(nslab, 1, gps * p), (nslab, nb, gps * p))

    lam_t = jnp.concatenate([lam_rows(jnp.real(lam_bar)), lam_rows(jnp.imag(lam_bar))], axis=0)
    return bw.astype(_BF), cw.astype(_BF), lam_t


def _lru_kernel(*refs, tt, n_chunks, n_ctx_chunks, reverse, final):
    if final:
        (rec_ref, prev_ref, next_ref, cw_ref, cb_ref, wg_ref, bg_ref, ap_ref, hb_ref, gate_ref,
         o_ref, a_ref, b_ref, h_ref) = refs
    else:
        (rec_ref, prev_ref, next_ref, cw_ref, cb_ref, wg_ref, bg_ref, ap_ref,
         o_ref, a_ref, b_ref, h_ref) = refs
    _, nb, lw = rec_ref.shape
    tm = tt * nb
    nblk, bwid, _ = wg_ref.shape
    i = pl.program_id(0)

    @pl.when(i == 0)
    def _():
        h_ref[...] = jnp.zeros_like(h_ref)

    c = _chunk_of(i, n_chunks, n_ctx_chunks, reverse)
    seg_start = jnp.logical_or(c == 0, c == n_ctx_chunks)
    seg_end = jnp.logical_or(c == n_ctx_chunks - 1, c == n_chunks - 1)
    prev = jnp.where(seg_start, 0.0, prev_ref[...])
    nxt = jnp.where(seg_end, 0.0, next_ref[...])
    ext = jnp.concatenate([prev, rec_ref[...], nxt], axis=0)
    u3 = cb_ref[...]
    for k in range(CONV_W):
        u3 = u3 + cw_ref[k:k + 1] * ext[k:k + tt]
    u = u3.reshape(tm, lw)
    ub = u.astype(_BF)
    sp = jax.nn.softplus(-ap_ref[...])
    for n in range(nblk):
        cols = slice(n * bwid, (n + 1) * bwid)
        g = _dot(ub[:, cols], wg_ref[n])
        r = jax.nn.sigmoid(g[:, 0:bwid] + bg_ref[0:1, cols])
        ig = jax.nn.sigmoid(g[:, bwid:2 * bwid] + bg_ref[1:2, cols])
        log_a = -LRU_C * r * sp[:, cols]
        a = jnp.exp(log_a)
        a_ref[:, :, cols] = a.reshape(tt, nb, bwid)
        bterm = jnp.sqrt(-jnp.tanh(log_a) * (1.0 + a * a)) * (ig * u[:, cols])
        b_ref[:, :, cols] = bterm.reshape(tt, nb, bwid)

    sc = 4 * LANES
    for c0 in range(0, lw, sc):

        def body(s, h, c0=c0):
            idx = tt - 1 - s if reverse else s
            h = a_ref[idx, :, c0:c0 + sc] * h + b_ref[idx, :, c0:c0 + sc]
            b_ref[idx, :, c0:c0 + sc] = h
            return h

        h_ref[:, c0:c0 + sc] = lax.fori_loop(0, tt, body, h_ref[:, c0:c0 + sc], unroll=4)

    if final:
        tot = (b_ref[...] + hb_ref[...]) * gate_ref[...]
        o_ref[...] = tot.reshape(tm, lw).astype(o_ref.dtype)
    else:
        o_ref[...] = b_ref[...]


def _lru_call(rec3, conv_w, conv_b, wg, bg, ap, hb, gate3, *, tt, ctx_len, reverse):
    ta, nb, lw = rec3.shape
    tm = tt * nb
    n_chunks = ta // tt
    nctx = ctx_len // tt
    final = hb is not None
    half = tt // 2

    def cidx(i):
        return _chunk_of(i, n_chunks, nctx, reverse)

    in_specs = [pl.BlockSpec((tt, nb, lw), lambda i: (cidx(i), 0, 0)),
                pl.BlockSpec((2, nb, lw), lambda i: (jnp.maximum(cidx(i) * half - 1, 0), 0, 0)),
                pl.BlockSpec((1, nb, lw), lambda i: (jnp.minimum((cidx(i) + 1) * tt, ta - 1), 0, 0)),
                pl.BlockSpec(conv_w.shape, lambda i: (0, 0)),
                pl.BlockSpec(conv_b.shape, lambda i: (0, 0)),
                pl.BlockSpec(wg.shape, lambda i: (0, 0, 0)),
                pl.BlockSpec(bg.shape, lambda i: (0, 0)),
                pl.BlockSpec(ap.shape, lambda i: (0, 0))]
    args = [rec3, rec3, rec3, conv_w, conv_b, wg, bg, ap]
    if final:
        in_specs += [pl.BlockSpec((tt, nb, lw), lambda i: (cidx(i), 0, 0)),
                     pl.BlockSpec((tt, nb, lw), lambda i: (cidx(i), 0, 0))]
        args += [hb, gate3]
        out_shape = jax.ShapeDtypeStruct((ta * nb, lw), _BF)
        out_spec = pl.BlockSpec((tm, lw), lambda i: (cidx(i), 0))
    else:
        out_shape = jax.ShapeDtypeStruct((ta, nb, lw), _F32)
        out_spec = pl.BlockSpec((tt, nb, lw), lambda i: (cidx(i), 0, 0))
    kern = functools.partial(_lru_kernel, tt=tt, n_chunks=n_chunks, n_ctx_chunks=nctx,
                             reverse=reverse, final=final)
    return pl.pallas_call(
        kern,
        out_shape=out_shape,
        grid=(n_chunks,),
        in_specs=in_specs,
        out_specs=out_spec,
        scratch_shapes=[pltpu.VMEM((tt, nb, lw), _F32),
                        pltpu.VMEM((tt, nb, lw), _F32),
                        pltpu.VMEM((nb, lw), _F32)],
        compiler_params=_cparams(1),
        name="lru_final" if final else "lru_first",
    )(*args)


def _rope_tables(seq, ctx_len, hd):
    n_freq = hd // 4
    pos = jnp.arange(seq)
    row = (pos // GRID_W).astype(_F32)
    col = (pos % GRID_W).astype(_F32)
    inv = ROPE_BASE ** (-jnp.arange(n_freq, dtype=_F32) / n_freq)
    ang = jnp.concatenate([row[:, None] * inv, col[:, None] * inv], axis=-1)
    cos, sin = jnp.cos(ang), jnp.sin(ang)
    cos_c = jnp.tile(jnp.concatenate([cos, cos], axis=-1), (1, LANES // hd))
    sin_c = jnp.tile(jnp.concatenate([-sin, sin], axis=-1), (1, LANES // hd))
    cos_c = jnp.concatenate([jnp.ones((ctx_len, LANES), _F32), cos_c], axis=0)
    sin_c = jnp.concatenate([jnp.zeros((ctx_len, LANES), _F32), sin_c], axis=0)
    return cos_c, sin_c


def kernel(x, c, ctx, c_ctx, ada_w, ada_b, norm_g, mlp_w1, mlp_w2, attn_w_qkv, attn_w_o, attn_lambda, attn_subln, s5_a_re, s5_a_im, s5_b_re, s5_b_im, s5_c_re, s5_c_im, s5_log_dt, s5_d, s5_w_glu, lru_w_in, lru_conv_w, lru_conv_b, lru_w_gate, lru_b_gate, lru_a_param, lru_w_out):
    nb, seq, d = x.shape
    ctx_len = ctx.shape[1]
    depth = ada_w.shape[0]
    assert nb % SUBLANES == 0 and d % LANES == 0
    assert depth % N_MIXERS == 1, "first and last layers must be attention layers (batch-major ends)"
    tt = 32
    tq = 256
    tf = 1024
    assert ctx_len % tq == 0 and seq % tq == 0 and ctx_len % tt == 0 and seq % tt == 0

    pad = (-(nb + 1)) % SUBLANES
    c_all = jnp.concatenate([c, c_ctx[None, :], jnp.zeros((pad, d), c.dtype)], axis=0)
    mod_all = _ada_call(c_all, ada_w.astype(_BF), ada_b)
    mod_lat = mod_all[:, :nb]
    mod_ctx = jnp.broadcast_to(mod_all[:, nb:nb + 1], mod_lat.shape)
    modtabs = jnp.stack([mod_ctx, mod_lat], axis=1)

    cos_t, sin_t = _rope_tables(seq, ctx_len, d // ATTN_HEADS // 2)
    w1b = mlp_w1.astype(_BF)
    w2b = mlp_w2.astype(_BF)

    xs = jnp.concatenate([ctx, x], axis=1)
    x_bm = True
    for i in range(depth):
        kind, j = i % N_MIXERS, i // N_MIXERS
        last = i == depth - 1
        modtab = modtabs[i]
        gain0 = norm_g[i, 0:1]
        if kind == 0:
            lambda_init = 0.8 - 0.6 * math.exp(-0.3 * i)
            qkv = _qkv_call(xs, modtab, gain0, attn_w_qkv[j].astype(_BF), cos_t, sin_t,
                            tt=tt, ctx_len=ctx_len, x_bm=x_bm)
            a = _attn_call(qkv, attn_lambda[j], attn_subln[j][None, :], ctx_len=ctx_len, tq=tq,
                           lambda_init=lambda_init)
            wa = attn_w_o[j].astype(_BF)
            glu, a_bm = False, True
        elif kind == 1:
            wts = [_s5_weights(s5_a_re[j, dr], s5_a_im[j, dr], s5_b_re[j, dr], s5_b_im[j, dr],
                               s5_c_re[j, dr], s5_c_im[j, dr], s5_log_dt[j, dr], nb)
                   for dr in range(2)]
            yb = _s5_call(xs, modtab, gain0, *wts[1], None, None, tt=tt, ctx_len=ctx_len,
                          reverse=True)
            a = _s5_call(xs, modtab, gain0, *wts[0], yb, s5_d[j][None, :], tt=tt,
                         ctx_len=ctx_len, reverse=False)
            wa = s5_w_glu[j].astype(_BF)
            glu, a_bm = True, False
        else:
            gate3, rec3 = _lru_in_call(xs, modtab, gain0, lru_w_in[j].astype(_BF), tt=tt,
                                       ctx_len=ctx_len)
            wg = [jnp.concatenate([lru_w_gate[j, dr, 0], lru_w_gate[j, dr, 1]], axis=-1).astype(_BF)
                  for dr in range(2)]
            cb = lru_conv_b[j][None, :]
            hb = _lru_call(rec3, lru_conv_w[j], cb, wg[1], lru_b_gate[j, 1],
                           lru_a_param[j, 1][None, :], None, None, tt=tt, ctx_len=ctx_len,
                           reverse=True)
            a = _lru_call(rec3, lru_conv_w[j], cb, wg[0], lru_b_gate[j, 0],
                          lru_a_param[j, 0][None, :], hb, gate3, tt=tt, ctx_len=ctx_len,
                          reverse=False)
            wa = lru_w_out[j].astype(_BF)
            glu, a_bm = False, False
        xs = _mlp_call(xs, a, wa, modtab, norm_g[i], w1b[i], w2b[i], tt=tt, tf=tf,
                       ctx_len=ctx_len, glu=glu, x_bm=x_bm, a_bm=a_bm, out_bm=last, skip_ctx=last)
        x_bm = last
    return xs
```

```python
import functools
import math

import jax
import jax.numpy as jnp
from jax import lax
from jax.experimental import pallas as pl
from jax.experimental.pallas import tpu as pltpu

EPS = 1e-6
GRID_W = 64
ROPE_BASE = 10000.0
N_MIXERS = 3
ATTN_HEADS = 8
S5_GROUP = 16
S5_STATE = 64
LRU_BW = 256
LRU_C = 8.0
CONV_W = 4
CONV_LEFT = 2

LANES = 128
SUBLANES = 8
VMEM_LIMIT = 56 * 1024 * 1024
MXU_TILE = 256

KB = MXU_TILE
RB = 8 * SUBLANES
STEP_UNROLL = 8
LOG2E = math.log2(math.e)

_BF = jnp.bfloat16
_F32 = jnp.float32


def _cparams(n_axes):
    return pltpu.CompilerParams(dimension_semantics=("arbitrary",) * n_axes,
                                vmem_limit_bytes=VMEM_LIMIT)


def _rms(x):
    return x * lax.rsqrt(jnp.mean(x * x, axis=-1, keepdims=True) + EPS)


def _normmod(x3, gain, scale, shift):
    return _rms(x3) * gain * (1.0 + scale) + shift


def _dot(a, b):
    return jnp.dot(a, b, preferred_element_type=_F32)


def _ada_kernel(c_ref, w_ref, b_ref, o_ref):
    sc = jax.nn.silu(c_ref[...]).astype(_BF)
    o_ref[0] = _dot(sc, w_ref[0]) + b_ref[0]


def _ada_call(c_all, ada_w, ada_b):
    depth, d, n = ada_w.shape
    rows = c_all.shape[0]
    tn = n // 4
    return pl.pallas_call(
        _ada_kernel,
        out_shape=jax.ShapeDtypeStruct((depth, rows, n), _F32),
        grid=(depth, n // tn),
        in_specs=[pl.BlockSpec((rows, d), lambda l, j: (0, 0)),
                  pl.BlockSpec((1, d, tn), lambda l, j: (l, 0, j)),
                  pl.BlockSpec((1, 1, tn), lambda l, j: (l, 0, j))],
        out_specs=pl.BlockSpec((1, rows, tn), lambda l, j: (l, 0, j)),
        compiler_params=_cparams(2),
        name="adaln",
    )(c_all, ada_w, ada_b.reshape(depth, 1, n))


def _rope_cols(t, cos, sin, lo_half):
    rot = jnp.where(lo_half, pltpu.roll(t, 96, 1), pltpu.roll(t, 32, 1))
    return t * cos + rot * sin


def _qkv_kernel(x_ref, mod_ref, g_ref, w_ref, cos_ref, sin_ref, o_ref, *, d, q_scale, x_bm):
    nb, tt, _ = o_ref.shape
    tm = tt * nb
    x3 = pltpu.einshape("btd->tbd", x_ref[...]) if x_bm else x_ref[...]
    mod = mod_ref[0]
    h = _normmod(x3, g_ref[...], mod[:, d:2 * d], mod[:, 0:d])
    hb = pltpu.einshape("tbd->btd", h.astype(_BF)).reshape(tm, d)
    cos = jnp.broadcast_to(cos_ref[...][None], (nb, tt, LANES)).reshape(tm, LANES)
    sin = jnp.broadcast_to(sin_ref[...][None], (nb, tt, LANES)).reshape(tm, LANES)
    lane = lax.broadcasted_iota(jnp.int32, (tm, LANES), 1)
    lo_half = (lane % 64) < 32
    n_rope = 2 * d // LANES
    cw = 4 * LANES
    for c0 in range(0, 3 * d, cw):
        acc = _dot(hb, w_ref[:, c0:c0 + cw])
        for k in range(cw // LANES):
            col = c0 // LANES + k
            t = acc[:, k * LANES:(k + 1) * LANES]
            if col < n_rope:
                t = _rope_cols(t, cos, sin, lo_half)
                if col < n_rope // 2:
                    t = t * q_scale
            o_ref[:, :, col * LANES:(col + 1) * LANES] = t.reshape(nb, tt, LANES).astype(o_ref.dtype)


def _qkv_call(x3, modtab, gain, w, cos_t, sin_t, *, tt, ctx_len, x_bm):
    if x_bm:
        nb, ta, d = x3.shape
        x_spec = pl.BlockSpec((nb, tt, d), lambda i: (0, i, 0))
    else:
        ta, nb, d = x3.shape
        x_spec = pl.BlockSpec((tt, nb, d), lambda i: (i, 0, 0))
    n = w.shape[1]
    nctx = ctx_len // tt
    q_scale = LOG2E * (d // ATTN_HEADS // 2) ** -0.5
    kern = functools.partial(_qkv_kernel, d=d, q_scale=q_scale, x_bm=x_bm)
    return pl.pallas_call(
        kern,
        out_shape=jax.ShapeDtypeStruct((nb, ta, n), _BF),
        grid=(ta // tt,),
        in_specs=[x_spec,
                  pl.BlockSpec((1, nb, 6 * d), lambda i: (jnp.where(i < nctx, 0, 1), 0, 0)),
                  pl.BlockSpec((1, d), lambda i: (0, 0)),
                  pl.BlockSpec((d, n), lambda i: (0, 0)),
                  pl.BlockSpec((tt, LANES), lambda i: (i, 0)),
                  pl.BlockSpec((tt, LANES), lambda i: (i, 0))],
        out_specs=pl.BlockSpec((nb, tt, n), lambda i: (0, i, 0)),
        compiler_params=_cparams(1),
        name="qkv_rope",
    )(x3, modtab, gain, w, cos_t, sin_t)


def _lru_in_kernel(x_ref, mod_ref, g_ref, w_ref, gate_ref, rec_ref, *, d, lw):
    tt, nb, _ = x_ref.shape
    tm = tt * nb
    mod = mod_ref[0]
    h = _normmod(x_ref[...], g_ref[...], mod[:, d:2 * d], mod[:, 0:d])
    hb = h.reshape(tm, d).astype(_BF)
    cw = 4 * LANES
    for c0 in range(0, lw, cw):
        acc = _dot(hb, w_ref[:, c0:c0 + cw])
        gate_ref[:, :, c0:c0 + cw] = jax.nn.gelu(acc).reshape(tt, nb, cw)
    for c0 in range(0, lw, cw):
        acc = _dot(hb, w_ref[:, lw + c0:lw + c0 + cw])
        rec_ref[:, :, c0:c0 + cw] = acc.reshape(tt, nb, cw)


def _lru_in_call(x3, modtab, gain, w, *, tt, ctx_len):
    ta, nb, d = x3.shape
    lw = w.shape[1] // 2
    nctx = ctx_len // tt
    kern = functools.partial(_lru_in_kernel, d=d, lw=lw)
    return pl.pallas_call(
        kern,
        out_shape=(jax.ShapeDtypeStruct((ta, nb, lw), _F32),
                   jax.ShapeDtypeStruct((ta, nb, lw), _F32)),
        grid=(ta // tt,),
        in_specs=[pl.BlockSpec((tt, nb, d), lambda i: (i, 0, 0)),
                  pl.BlockSpec((1, nb, 6 * d), lambda i: (jnp.where(i < nctx, 0, 1), 0, 0)),
                  pl.BlockSpec((1, d), lambda i: (0, 0)),
                  pl.BlockSpec((d, 2 * lw), lambda i: (0, 0))],
        out_specs=(pl.BlockSpec((tt, nb, lw), lambda i: (i, 0, 0)),
                   pl.BlockSpec((tt, nb, lw), lambda i: (i, 0, 0))),
        compiler_params=_cparams(1),
        name="lru_in",
    )(x3, modtab, gain, w)


def _attn_kernel(q_ref, k_ref, v_ref, lam_ref, sg_ref, o_ref,
                 s_ref, va_ref, q2_ref, mx_ref, mf_ref, oa_ref, *, tq, ctx_len, lambda_init):
    ta, hw = k_ref.shape
    hd = hw // 2
    n_ctx = ctx_len // tq
    n_lat = ta // tq - n_ctx
    nkb = ta // KB
    ctx_kb = ctx_len // KB
    lane = lax.broadcasted_iota(jnp.int32, (tq, hw), 1)
    lv = lam_ref[...]
    lam = (jnp.exp(jnp.sum(lv[0:1] * lv[1:2], axis=-1, keepdims=True))
           - jnp.exp(jnp.sum(lv[2:3] * lv[3:4], axis=-1, keepdims=True)) + lambda_init)
    out_scale = sg_ref[...] * (1.0 - lambda_init)

    va_ref[:, 0:hw] = v_ref[...]
    va_ref[:, hw:2 * hw] = jnp.ones((ta, hw), _BF)

    def stack_q(r0):
        q = q_ref[pl.ds(r0, tq), :]
        zero = jnp.zeros_like(q)
        return jnp.concatenate([jnp.where(lane < hd, q, zero), jnp.where(lane < hd, zero, q)], axis=0)

    def nt_dot(a, b):
        return lax.dot_general(a, b, (((1,), (1,)), ((), ())), preferred_element_type=_F32)

    def combine(acc):
        o1 = acc[0:tq, 0:hw] / acc[0:tq, hw:2 * hw]
        o2 = acc[tq:2 * tq, 0:hw] / acc[tq:2 * tq, hw:2 * hw]
        o = o1 - lam * o2
        return (_rms(o) * out_scale).astype(o_ref.dtype)

    for c in range(n_ctx):
        s = nt_dot(stack_q(c * tq), k_ref[0:ctx_len, :])
        e = jnp.exp2(s - jnp.max(s, axis=-1, keepdims=True))
        o_ref[c * tq:(c + 1) * tq, :] = combine(_dot(e.astype(_BF), va_ref[0:ctx_len, :]))

    def block_step(j0, nj, par, do_a, do_b):
        k0 = pl.multiple_of(j0 * KB, KB) if not isinstance(j0, int) else j0 * KB
        if do_a:
            sa = nt_dot(q2_ref[...], k_ref[pl.ds(k0, nj * KB), :])
            for rb in range(2 * tq // RB):
                rows = slice(rb * RB, (rb + 1) * RB)
                mx = mx_ref[par, rows, :]
                for u in range(nj):
                    blk = sa[rows, u * KB:(u + 1) * KB]
                    s_ref[par, j0 + u, rows, :] = blk
                    for l0 in range(0, KB, LANES):
                        mx = jnp.maximum(mx, blk[:, l0:l0 + LANES])
                mx_ref[par, rows, :] = mx
        if do_b:
            e_rows = []
            for rb in range(2 * tq // RB):
                rows = slice(rb * RB, (rb + 1) * RB)
                mf = mf_ref[1 - par, rows, :]
                parts = []
                for u in range(nj):
                    sb = s_ref[1 - par, j0 + u, rows, :]
                    for l0 in range(0, KB, LANES):
                        parts.append(jnp.exp2(sb[:, l0:l0 + LANES] - mf).astype(_BF))
                e_rows.append(jnp.concatenate(parts, axis=1))
            e = jnp.concatenate(e_rows, axis=0)
            oa_ref[...] += _dot(e, va_ref[pl.ds(k0, nj * KB), :])

    def sweep(tau, par, do_a, do_b):
        if do_a:
            r0 = (n_ctx + tau) * tq
            q2_ref[...] = stack_q(r0 if isinstance(r0, int) else pl.multiple_of(r0, tq))
            mx_ref[par] = jnp.full(mx_ref.shape[1:], -jnp.inf, _F32)
        if do_b:
            oa_ref[...] = jnp.zeros_like(oa_ref)
        step = functools.partial(block_step, par=par, do_a=do_a, do_b=do_b)
        for j in range(ctx_kb):
            step(j, 1)
        n_pairs = (nkb - ctx_kb) // 2
        assert (nkb - ctx_kb) % 2 == 0
        unroll = math.gcd(n_pairs, STEP_UNROLL)

        def body(c, carry):
            for u in range(unroll):
                step(ctx_kb + 2 * (unroll * c + u), 2)
            return carry

        if n_pairs == unroll:
            body(0, 0)
        else:
            lax.fori_loop(0, n_pairs // unroll, body, 0)
        if do_a:
            m = jnp.max(mx_ref[par], axis=-1, keepdims=True)
            mf_ref[par] = jnp.broadcast_to(m, mf_ref.shape[1:])
        if do_b:
            r0 = (n_ctx + tau - 1) * tq
            r0 = r0 if isinstance(r0, int) else pl.multiple_of(r0, tq)
            o_ref[pl.ds(r0, tq), :] = combine(oa_ref[...])

    assert n_lat >= 2 and n_lat % 2 == 0
    sweep(0, 0, True, False)

    def pair(p, carry):
        sweep(2 * p + 1, 1, True, True)
        sweep(2 * p + 2, 0, True, True)
        return carry

    lax.fori_loop(0, (n_lat - 2) // 2, pair, 0)
    sweep(n_lat - 1, 1, True, True)
    sweep(n_lat, 0, False, True)


def _attn_call(qkv, lam_vecs, subln, *, ctx_len, tq, lambda_init):
    nb, ta, n = qkv.shape
    d = n // 3
    hw = d // ATTN_HEADS
    assert ta % KB == 0 and ctx_len % KB == 0 and tq % RB == 0
    nkb = ta // KB
    kern = functools.partial(_attn_kernel, tq=tq, ctx_len=ctx_len, lambda_init=lambda_init)
    return pl.pallas_call(
        kern,
        out_shape=jax.ShapeDtypeStruct((nb, ta, d), _BF),
        grid=(nb, ATTN_HEADS),
        in_specs=[pl.BlockSpec((None, ta, hw), lambda b, h: (b, 0, h)),
                  pl.BlockSpec((None, ta, hw), lambda b, h: (b, 0, ATTN_HEADS + h)),
                  pl.BlockSpec((None, ta, hw), lambda b, h: (b, 0, 2 * ATTN_HEADS + h)),
                  pl.BlockSpec(lam_vecs.shape, lambda b, h: (0, 0)),
                  pl.BlockSpec((1, hw), lambda b, h: (0, 0))],
        out_specs=pl.BlockSpec((None, ta, hw), lambda b, h: (b, 0, h)),
        scratch_shapes=[pltpu.VMEM((2, nkb, 2 * tq, KB), _F32),
                        pltpu.VMEM((ta, 2 * hw), _BF),
                        pltpu.VMEM((2 * tq, hw), _BF),
                        pltpu.VMEM((2, 2 * tq, LANES), _F32),
                        pltpu.VMEM((2, 2 * tq, LANES), _F32),
                        pltpu.VMEM((2 * tq, 2 * hw), _F32)],
        compiler_params=_cparams(2),
        name="diff_attn",
    )(qkv, qkv, qkv, lam_vecs, subln)


def _mlp_kernel(x_ref, a_ref, wa_ref, mod_ref, ng_ref, w1_ref, w2_ref, o_ref,
                x1_ref, h2_ref, acc_ref, *, d, glu, x_bm, a_bm, out_bm):
    tt, nb, _ = x1_ref.shape
    tm = tt * nb
    f = pl.program_id(1)

    @pl.when(f == 0)
    def _():
        mod = mod_ref[0]
        a = a_ref[...]
        if a_bm:
            a = a.reshape(tm, a.shape[-1])
        y = _dot(a, wa_ref[...])
        if glu:
            y = y[:, 0:d] * jax.nn.sigmoid(y[:, d:2 * d])
        y3 = pltpu.einshape("btd->tbd", y.reshape(nb, tt, d)) if a_bm else y.reshape(tt, nb, d)
        x3 = pltpu.einshape("btd->tbd", x_ref[...]) if x_bm else x_ref[...]
        x1 = x3 + mod[:, 2 * d:3 * d] * (_rms(y3) * ng_ref[1:2])
        x1_ref[...] = x1
        h2 = _normmod(x1, ng_ref[2:3], mod[:, 4 * d:5 * d], mod[:, 3 * d:4 * d])
        h2_ref[...] = h2.reshape(tm, d).astype(_BF)
        acc_ref[...] = jnp.zeros_like(acc_ref)

    t = _dot(h2_ref[...], w1_ref[...])
    t = jnp.maximum(t, 0.0)
    t = (t * t).astype(_BF)
    acc_ref[...] += _dot(t, w2_ref[...])

    @pl.when(f == pl.num_programs(1) - 1)
    def _():
        mod = mod_ref[0]
        m3 = acc_ref[...].reshape(tt, nb, d)
        out = x1_ref[...] + mod[:, 5 * d:6 * d] * (_rms(m3) * ng_ref[3:4])
        o_ref[...] = pltpu.einshape("tbd->btd", out) if out_bm else out


def _mlp_call(x, a, wa, modtab, ng, w1, w2, *, tt, tf, ctx_len, glu, x_bm, a_bm, out_bm, skip_ctx):
    if x_bm:
        nb, ta, d = x.shape
    else:
        ta, nb, d = x.shape
    tm = tt * nb
    ka, na = wa.shape
    dff = w1.shape[1]
    nctx = ctx_len // tt
    t_off = nctx if skip_ctx else 0
    n_t = ta // tt - t_off
    x_spec = (pl.BlockSpec((nb, tt, d), lambda i, f: (0, i + t_off, 0)) if x_bm
              else pl.BlockSpec((tt, nb, d), lambda i, f: (i + t_off, 0, 0)))
    a_spec = (pl.BlockSpec((nb, tt, ka), lambda i, f: (0, i + t_off, 0)) if a_bm
              else pl.BlockSpec((tm, ka), lambda i, f: (i + t_off, 0)))
    if out_bm:
        out_shape = jax.ShapeDtypeStruct((nb, n_t * tt, d), _F32)
        out_spec = pl.BlockSpec((nb, tt, d), lambda i, f: (0, i, 0))
    else:
        out_shape = jax.ShapeDtypeStruct((n_t * tt, nb, d), _F32)
        out_spec = pl.BlockSpec((tt, nb, d), lambda i, f: (i, 0, 0))
    kern = functools.partial(_mlp_kernel, d=d, glu=glu, x_bm=x_bm, a_bm=a_bm, out_bm=out_bm)
    return pl.pallas_call(
        kern,
        out_shape=out_shape,
        grid=(n_t, dff // tf),
        in_specs=[x_spec,
                  a_spec,
                  pl.BlockSpec((ka, na), lambda i, f: (0, 0)),
                  pl.BlockSpec((1, nb, 6 * d), lambda i, f: (jnp.where(i + t_off < nctx, 0, 1), 0, 0)),
                  pl.BlockSpec((4, d), lambda i, f: (0, 0)),
                  pl.BlockSpec((d, tf), lambda i, f: (0, f)),
                  pl.BlockSpec((tf, d), lambda i, f: (f, 0))],
        out_specs=out_spec,
        scratch_shapes=[pltpu.VMEM((tt, nb, d), _F32),
                        pltpu.VMEM((tm, d), _BF),
                        pltpu.VMEM((tm, d), _F32)],
        compiler_params=_cparams(2),
        name="proj_mlp",
    )(x, a, wa, modtab, ng, w1, w2)


def _chunk_of(i, n_chunks, n_ctx_chunks, reverse):
    if not reverse:
        return i
    return jnp.where(i < n_ctx_chunks, n_ctx_chunks - 1 - i, n_chunks - 1 - (i - n_ctx_chunks))


def _s5_kernel(*refs, d, reverse, final):
    if final:
        (x_ref, mod_ref, g_ref, bw_ref, cw_ref, lam_ref, yb_ref, dsk_ref,
         o_ref, bu_ref, h_ref) = refs
    else:
        x_ref, mod_ref, g_ref, bw_ref, cw_ref, lam_ref, o_ref, bu_ref, h_ref = refs
    tt, nb, _ = x_ref.shape
    tm = tt * nb
    nslab = d // LANES
    sw = bw_ref.shape[2]
    hw = sw // 2

    @pl.when(pl.program_id(0) == 0)
    def _():
        h_ref[...] = jnp.zeros_like(h_ref)

    mod = mod_ref[0]
    u3 = _normmod(x_ref[...], g_ref[...], mod[:, d:2 * d], mod[:, 0:d])
    ub = u3.reshape(tm, d).astype(_BF)
    for j in range(nslab):
        bu = _dot(ub[:, j * LANES:(j + 1) * LANES], bw_ref[j])
        bu_ref[:, :, j * sw:(j + 1) * sw] = bu.reshape(tt, nb, sw)

    for j in range(nslab):
        cr = j * sw
        ci = cr + hw
        lr = lam_ref[j]
        li = lam_ref[nslab + j]

        def body(s, carry, cr=cr, ci=ci, lr=lr, li=li):
            hr, hi = carry
            idx = tt - 1 - s if reverse else s
            nr = lr * hr - li * hi + bu_ref[idx, :, cr:cr + hw]
            ni = lr * hi + li * hr + bu_ref[idx, :, ci:ci + hw]
            bu_ref[idx, :, cr:cr + hw] = nr
            bu_ref[idx, :, ci:ci + hw] = ni
            return nr, ni

        hr, hi = lax.fori_loop(0, tt, body, (h_ref[:, cr:cr + hw], h_ref[:, ci:ci + hw]),
                               unroll=4)
        h_ref[:, cr:cr + hw] = hr
        h_ref[:, ci:ci + hw] = hi

    for j in range(nslab):
        hs = bu_ref[:, :, j * sw:(j + 1) * sw].reshape(tm, sw).astype(_BF)
        y = _dot(hs, cw_ref[j]).reshape(tt, nb, LANES)
        cols = slice(j * LANES, (j + 1) * LANES)
        if final:
            tot = y + yb_ref[:, :, cols] + dsk_ref[:, cols] * u3[:, :, cols]
            o_ref[:, cols] = jax.nn.gelu(tot).reshape(tm, LANES).astype(o_ref.dtype)
        else:
            o_ref[:, :, cols] = y


def _s5_call(x3, modtab, gain, bw, cw, lam, yb, dskip, *, tt, ctx_len, reverse):
    ta, nb, d = x3.shape
    tm = tt * nb
    n_chunks = ta // tt
    nctx = ctx_len // tt
    final = yb is not None
    nslab, _, sw = bw.shape

    def cidx(i):
        return _chunk_of(i, n_chunks, nctx, reverse)

    in_specs = [pl.BlockSpec((tt, nb, d), lambda i: (cidx(i), 0, 0)),
                pl.BlockSpec((1, nb, 6 * d), lambda i: (jnp.where(cidx(i) < nctx, 0, 1), 0, 0)),
                pl.BlockSpec((1, d), lambda i: (0, 0)),
                pl.BlockSpec(bw.shape, lambda i: (0, 0, 0)),
                pl.BlockSpec(cw.shape, lambda i: (0, 0, 0)),
                pl.BlockSpec(lam.shape, lambda i: (0, 0, 0))]
    args = [x3, modtab, gain, bw, cw, lam]
    if final:
        in_specs += [pl.BlockSpec((tt, nb, d), lambda i: (cidx(i), 0, 0)),
                     pl.BlockSpec((1, d), lambda i: (0, 0))]
        args += [yb, dskip]
        out_shape = jax.ShapeDtypeStruct((ta * nb, d), _BF)
        out_spec = pl.BlockSpec((tm, d), lambda i: (cidx(i), 0))
    else:
        out_shape = jax.ShapeDtypeStruct((ta, nb, d), _F32)
        out_spec = pl.BlockSpec((tt, nb, d), lambda i: (cidx(i), 0, 0))
    kern = functools.partial(_s5_kernel, d=d, reverse=reverse, final=final)
    return pl.pallas_call(
        kern,
        out_shape=out_shape,
        grid=(n_chunks,),
        in_specs=in_specs,
        out_specs=out_spec,
        scratch_shapes=[pltpu.VMEM((tt, nb, nslab * sw), _F32),
                        pltpu.VMEM((nb, nslab * sw), _F32)],
        compiler_params=_cparams(1),
        name="s5_final" if final else "s5_first",
    )(*args)


def _s5_weights(a_re, a_im, b_re, b_im, c_re, c_im, log_dt, nb):
    ng, p = a_re.shape
    gc = b_re.shape[-1]
    gps = LANES // gc
    nslab = ng // gps
    lam = lax.complex(a_re.astype(_F32), a_im.astype(_F32))
    dt = jnp.exp(log_dt.astype(_F32))[:, None]
    lam_bar = jnp.exp(lam * dt)
    b_mat = lax.complex(b_re.astype(_F32), b_im.astype(_F32))
    b_bar = ((lam_bar - 1.0) / lam)[..., None] * b_mat
    eye = jnp.eye(gps, dtype=_F32)

    def b_block(part):
        blk = part.reshape(nslab, gps, p, gc)
        return jnp.einsum('jgpc,gh->jgchp', blk, eye).reshape(nslab, gps * gc, gps * p)

    bw = jnp.concatenate([b_block(jnp.real(b_bar)), b_block(jnp.imag(b_bar))], axis=-1)

    def c_block(part):
        blk = part.reshape(nslab, gps, gc, p)
        return jnp.einsum('jgcp,gh->jgphc', blk, eye).reshape(nslab, gps * p, gps * gc)

    cw = jnp.concatenate([c_block(c_re.astype(_F32)), c_block(-c_im.astype(_F32))], axis=1)

    def lam_rows(part):
        return jnp.broadcast_to(part.reshape(nslab, 1, gps * p), (nslab, nb, gps * p))

    lam_t = jnp.concatenate([lam_rows(jnp.real(lam_bar)), lam_rows(jnp.imag(lam_bar))], axis=0)
    return bw.astype(_BF), cw.astype(_BF), lam_t


def _lru_kernel(*refs, tt, n_chunks, n_ctx_chunks, reverse, final):
    if final:
        (rec_ref, prev_ref, next_ref, cw_ref, cb_ref, wg_ref, bg_ref, ap_ref, hb_ref, gate_ref,
         o_ref, a_ref, b_ref, h_ref) = refs
    else:
        (rec_ref, prev_ref, next_ref, cw_ref, cb_ref, wg_ref, bg_ref, ap_ref,
         o_ref, a_ref, b_ref, h_ref) = refs
    _, nb, lw = rec_ref.shape
    tm = tt * nb
    nblk, bwid, _ = wg_ref.shape
    i = pl.program_id(0)

    @pl.when(i == 0)
    def _():
        h_ref[...] = jnp.zeros_like(h_ref)

    c = _chunk_of(i, n_chunks, n_ctx_chunks, reverse)
    seg_start = jnp.logical_or(c == 0, c == n_ctx_chunks)
    seg_end = jnp.logical_or(c == n_ctx_chunks - 1, c == n_chunks - 1)
    prev = jnp.where(seg_start, 0.0, prev_ref[...])
    nxt = jnp.where(seg_end, 0.0, next_ref[...])
    ext = jnp.concatenate([prev, rec_ref[...], nxt], axis=0)
    u3 = cb_ref[...]
    for k in range(CONV_W):
        u3 = u3 + cw_ref[k:k + 1] * ext[k:k + tt]
    u = u3.reshape(tm, lw)
    ub = u.astype(_BF)
    sp = jax.nn.softplus(-ap_ref[...])
    for n in range(nblk):
        cols = slice(n * bwid, (n + 1) * bwid)
        g = _dot(ub[:, cols], wg_ref[n])
        r = jax.nn.sigmoid(g[:, 0:bwid] + bg_ref[0:1, cols])
        ig = jax.nn.sigmoid(g[:, bwid:2 * bwid] + bg_ref[1:2, cols])
        log_a = -LRU_C * r * sp[:, cols]
        a = jnp.exp(log_a)
        a_ref[:, :, cols] = a.reshape(tt, nb, bwid)
        bterm = jnp.sqrt(-jnp.tanh(log_a) * (1.0 + a * a)) * (ig * u[:, cols])
        b_ref[:, :, cols] = bterm.reshape(tt, nb, bwid)

    sc = 4 * LANES
    for c0 in range(0, lw, sc):

        def body(s, h, c0=c0):
            idx = tt - 1 - s if reverse else s
            h = a_ref[idx, :, c0:c0 + sc] * h + b_ref[idx, :, c0:c0 + sc]
            b_ref[idx, :, c0:c0 + sc] = h
            return h

        h_ref[:, c0:c0 + sc] = lax.fori_loop(0, tt, body, h_ref[:, c0:c0 + sc], unroll=4)

    if final:
        tot = (b_ref[...] + hb_ref[...]) * gate_ref[...]
        o_ref[...] = tot.reshape(tm, lw).astype(o_ref.dtype)
    else:
        o_ref[...] = b_ref[...]


def _lru_call(rec3, conv_w, conv_b, wg, bg, ap, hb, gate3, *, tt, ctx_len, reverse):
    ta, nb, lw = rec3.shape
    tm = tt * nb
    n_chunks = ta // tt
    nctx = ctx_len // tt
    final = hb is not None
    half = tt // 2

    def cidx(i):
        return _chunk_of(i, n_chunks, nctx, reverse)

    in_specs = [pl.BlockSpec((tt, nb, lw), lambda i: (cidx(i), 0, 0)),
                pl.BlockSpec((2, nb, lw), lambda i: (jnp.maximum(cidx(i) * half - 1, 0), 0, 0)),
                pl.BlockSpec((1, nb, lw), lambda i: (jnp.minimum((cidx(i) + 1) * tt, ta - 1), 0, 0)),
                pl.BlockSpec(conv_w.shape, lambda i: (0, 0)),
                pl.BlockSpec(conv_b.shape, lambda i: (0, 0)),
                pl.BlockSpec(wg.shape, lambda i: (0, 0, 0)),
                pl.BlockSpec(bg.shape, lambda i: (0, 0)),
                pl.BlockSpec(ap.shape, lambda i: (0, 0))]
    args = [rec3, rec3, rec3, conv_w, conv_b, wg, bg, ap]
    if final:
        in_specs += [pl.BlockSpec((tt, nb, lw), lambda i: (cidx(i), 0, 0)),
                     pl.BlockSpec((tt, nb, lw), lambda i: (cidx(i), 0, 0))]
        args += [hb, gate3]
        out_shape = jax.ShapeDtypeStruct((ta * nb, lw), _BF)
        out_spec = pl.BlockSpec((tm, lw), lambda i: (cidx(i), 0))
    else:
        out_shape = jax.ShapeDtypeStruct((ta, nb, lw), _F32)
        out_spec = pl.BlockSpec((tt, nb, lw), lambda i: (cidx(i), 0, 0))
    kern = functools.partial(_lru_kernel, tt=tt, n_chunks=n_chunks, n_ctx_chunks=nctx,
                             reverse=reverse, final=final)
    return pl.pallas_call(
        kern,
        out_shape=out_shape,
        grid=(n_chunks,),
        in_specs=in_specs,
        out_specs=out_spec,
        scratch_shapes=[pltpu.VMEM((tt, nb, lw), _F32),
                        pltpu.VMEM((tt, nb, lw), _F32),
                        pltpu.VMEM((nb, lw), _F32)],
        compiler_params=_cparams(1),
        name="lru_final" if final else "lru_first",
    )(*args)


def _rope_tables(seq, ctx_len, hd):
    n_freq = hd // 4
    pos = jnp.arange(seq)
    row = (pos // GRID_W).astype(_F32)
    col = (pos % GRID_W).astype(_F32)
    inv = ROPE_BASE ** (-jnp.arange(n_freq, dtype=_F32) / n_freq)
    ang = jnp.concatenate([row[:, None] * inv, col[:, None] * inv], axis=-1)
    cos, sin = jnp.cos(ang), jnp.sin(ang)
    cos_c = jnp.tile(jnp.concatenate([cos, cos], axis=-1), (1, LANES // hd))
    sin_c = jnp.tile(jnp.concatenate([-sin, sin], axis=-1), (1, LANES // hd))
    cos_c = jnp.concatenate([jnp.ones((ctx_len, LANES), _F32), cos_c], axis=0)
    sin_c = jnp.concatenate([jnp.zeros((ctx_len, LANES), _F32), sin_c], axis=0)
    return cos_c, sin_c


def kernel(x, c, ctx, c_ctx, ada_w, ada_b, norm_g, mlp_w1, mlp_w2, attn_w_qkv, attn_w_o, attn_lambda, attn_subln, s5_a_re, s5_a_im, s5_b_re, s5_b_im, s5_c_re, s5_c_im, s5_log_dt, s5_d, s5_w_glu, lru_w_in, lru_conv_w, lru_conv_b, lru_w_gate, lru_b_gate, lru_a_param, lru_w_out):
    nb, seq, d = x.shape
    ctx_len = ctx.shape[1]
    depth = ada_w.shape[0]
    assert nb % SUBLANES == 0 and d % LANES == 0
    assert depth % N_MIXERS == 1, "first and last layers must be attention layers (batch-major ends)"
    tt = 32
    tq = 256
    tf = 1024
    assert ctx_len % tq == 0 and seq % tq == 0 and ctx_len % tt == 0 and seq % tt == 0

    pad = (-(nb + 1)) % SUBLANES
    c_all = jnp.concatenate([c, c_ctx[None, :], jnp.zeros((pad, d), c.dtype)], axis=0)
    mod_all = _ada_call(c_all, ada_w.astype(_BF), ada_b)
    mod_lat = mod_all[:, :nb]
    mod_ctx = jnp.broadcast_to(mod_all[:, nb:nb + 1], mod_lat.shape)
    modtabs = jnp.stack([mod_ctx, mod_lat], axis=1)

    cos_t, sin_t = _rope_tables(seq, ctx_len, d // ATTN_HEADS // 2)
    w1b = mlp_w1.astype(_BF)
    w2b = mlp_w2.astype(_BF)

    xs = jnp.concatenate([ctx, x], axis=1)
    x_bm = True
    for i in range(depth):
        kind, j = i % N_MIXERS, i // N_MIXERS
        last = i == depth - 1
        modtab = modtabs[i]
        gain0 = norm_g[i, 0:1]
        if kind == 0:
            lambda_init = 0.8 - 0.6 * math.exp(-0.3 * i)
            qkv = _qkv_call(xs, modtab, gain0, attn_w_qkv[j].astype(_BF), cos_t, sin_t,
                            tt=tt, ctx_len=ctx_len, x_bm=x_bm)
            a = _attn_call(qkv, attn_lambda[j], attn_subln[j][None, :], ctx_len=ctx_len, tq=tq,
                           lambda_init=lambda_init)
            wa = attn_w_o[j].astype(_BF)
            glu, a_bm = False, True
        elif kind == 1:
            wts = [_s5_weights(s5_a_re[j, dr], s5_a_im[j, dr], s5_b_re[j, dr], s5_b_im[j, dr],
                               s5_c_re[j, dr], s5_c_im[j, dr], s5_log_dt[j, dr], nb)
                   for dr in range(2)]
            yb = _s5_call(xs, modtab, gain0, *wts[1], None, None, tt=tt, ctx_len=ctx_len,
                          reverse=True)
            a = _s5_call(xs, modtab, gain0, *wts[0], yb, s5_d[j][None, :], tt=tt,
                         ctx_len=ctx_len, reverse=False)
            wa = s5_w_glu[j].astype(_BF)
            glu, a_bm = True, False
        else:
            gate3, rec3 = _lru_in_call(xs, modtab, gain0, lru_w_in[j].astype(_BF), tt=tt,
                                       ctx_len=ctx_len)
            wg = [jnp.concatenate([lru_w_gate[j, dr, 0], lru_w_gate[j, dr, 1]], axis=-1).astype(_BF)
                  for dr in range(2)]
            cb = lru_conv_b[j][None, :]
            hb = _lru_call(rec3, lru_conv_w[j], cb, wg[1], lru_b_gate[j, 1],
                           lru_a_param[j, 1][None, :], None, None, tt=tt, ctx_len=ctx_len,
                           reverse=True)
            a = _lru_call(rec3, lru_conv_w[j], cb, wg[0], lru_b_gate[j, 0],
                          lru_a_param[j, 0][None, :], hb, gate3, tt=tt, ctx_len=ctx_len,
                          reverse=False)
            wa = lru_w_out[j].astype(_BF)
            glu, a_bm = False, False
        xs = _mlp_call(xs, a, wa, modtab, norm_g[i], w1b[i], w2b[i], tt=tt, tf=tf,
                       ctx_len=ctx_len, glu=glu, x_bm=x_bm, a_bm=a_bm, out_bm=last, skip_ctx=last)
        x_bm = last
    return xs
```

```python
import functools
import math

import jax
import jax.numpy as jnp
from jax import lax
from jax.experimental import pallas as pl
from jax.experimental.pallas import tpu as pltpu

EPS = 1e-6
GRID_W = 64
ROPE_BASE = 10000.0
N_MIXERS = 3
ATTN_HEADS = 8
S5_GROUP = 16
S5_STATE = 64
LRU_BW = 256
LRU_C = 8.0
CONV_W = 4
CONV_LEFT = 2

LANES = 128
SUBLANES = 8
VMEM_LIMIT = 56 * 1024 * 1024
MXU_TILE = 256

KB = MXU_TILE
RB = 8 * SUBLANES
STEP_UNROLL = 8
LOG2E = math.log2(math.e)

_BF = jnp.bfloat16
_F32 = jnp.float32


def _cparams(n_axes):
    return pltpu.CompilerParams(dimension_semantics=("arbitrary",) * n_axes,
                                vmem_limit_bytes=VMEM_LIMIT)


def _rms(x):
    return x * lax.rsqrt(jnp.mean(x * x, axis=-1, keepdims=True) + EPS)


def _normmod(x3, gain, scale, shift):
    return _rms(x3) * gain * (1.0 + scale) + shift


def _dot(a, b):
    return jnp.dot(a, b, preferred_element_type=_F32)


def _ada_kernel(c_ref, w_ref, b_ref, o_ref):
    sc = jax.nn.silu(c_ref[...]).astype(_BF)
    o_ref[0] = _dot(sc, w_ref[0]) + b_ref[0]


def _ada_call(c_all, ada_w, ada_b):
    depth, d, n = ada_w.shape
    rows = c_all.shape[0]
    tn = n // 4
    return pl.pallas_call(
        _ada_kernel,
        out_shape=jax.ShapeDtypeStruct((depth, rows, n), _F32),
        grid=(depth, n // tn),
        in_specs=[pl.BlockSpec((rows, d), lambda l, j: (0, 0)),
                  pl.BlockSpec((1, d, tn), lambda l, j: (l, 0, j)),
                  pl.BlockSpec((1, 1, tn), lambda l, j: (l, 0, j))],
        out_specs=pl.BlockSpec((1, rows, tn), lambda l, j: (l, 0, j)),
        compiler_params=_cparams(2),
        name="adaln",
    )(c_all, ada_w, ada_b.reshape(depth, 1, n))


def _rope_cols(t, cos, sin, lo_half):
    rot = jnp.where(lo_half, pltpu.roll(t, 96, 1), pltpu.roll(t, 32, 1))
    return t * cos + rot * sin


def _qkv_kernel(x_ref, mod_ref, g_ref, w_ref, cos_ref, sin_ref, o_ref, *, d, q_scale, x_bm):
    nb, tt, _ = o_ref.shape
    tm = tt * nb
    x3 = pltpu.einshape("btd->tbd", x_ref[...]) if x_bm else x_ref[...]
    mod = mod_ref[0]
    h = _normmod(x3, g_ref[...], mod[:, d:2 * d], mod[:, 0:d])
    hb = pltpu.einshape("tbd->btd", h.astype(_BF)).reshape(tm, d)
    cos = jnp.broadcast_to(cos_ref[...][None], (nb, tt, LANES)).reshape(tm, LANES)
    sin = jnp.broadcast_to(sin_ref[...][None], (nb, tt, LANES)).reshape(tm, LANES)
    lane = lax.broadcasted_iota(jnp.int32, (tm, LANES), 1)
    lo_half = (lane % 64) < 32
    n_rope = 2 * d // LANES
    cw = 4 * LANES
    for c0 in range(0, 3 * d, cw):
        acc = _dot(hb, w_ref[:, c0:c0 + cw])
        for k in range(cw // LANES):
            col = c0 // LANES + k
            t = acc[:, k * LANES:(k + 1) * LANES]
            if col < n_rope:
                t = _rope_cols(t, cos, sin, lo_half)
                if col < n_rope // 2:
                    t = t * q_scale
            o_ref[:, :, col * LANES:(col + 1) * LANES] = t.reshape(nb, tt, LANES).astype(o_ref.dtype)


def _qkv_call(x3, modtab, gain, w, cos_t, sin_t, *, tt, ctx_len, x_bm):
    if x_bm:
        nb, ta, d = x3.shape
        x_spec = pl.BlockSpec((nb, tt, d), lambda i: (0, i, 0))
    else:
        ta, nb, d = x3.shape
        x_spec = pl.BlockSpec((tt, nb, d), lambda i: (i, 0, 0))
    n = w.shape[1]
    nctx = ctx_len // tt
    q_scale = LOG2E * (d // ATTN_HEADS // 2) ** -0.5
    kern = functools.partial(_qkv_kernel, d=d, q_scale=q_scale, x_bm=x_bm)
    return pl.pallas_call(
        kern,
        out_shape=jax.ShapeDtypeStruct((nb, ta, n), _BF),
        grid=(ta // tt,),
        in_specs=[x_spec,
                  pl.BlockSpec((1, nb, 6 * d), lambda i: (jnp.where(i < nctx, 0, 1), 0, 0)),
                  pl.BlockSpec((1, d), lambda i: (0, 0)),
                  pl.BlockSpec((d, n), lambda i: (0, 0)),
                  pl.BlockSpec((tt, LANES), lambda i: (i, 0)),
                  pl.BlockSpec((tt, LANES), lambda i: (i, 0))],
        out_specs=pl.BlockSpec((nb, tt, n), lambda i: (0, i, 0)),
        compiler_params=_cparams(1),
        name="qkv_rope",
    )(x3, modtab, gain, w, cos_t, sin_t)


def _lru_in_kernel(x_ref, mod_ref, g_ref, w_ref, gate_ref, rec_ref, *, d, lw):
    tt, nb, _ = x_ref.shape
    tm = tt * nb
    mod = mod_ref[0]
    h = _normmod(x_ref[...], g_ref[...], mod[:, d:2 * d], mod[:, 0:d])
    hb = h.reshape(tm, d).astype(_BF)
    cw = 4 * LANES
    for c0 in range(0, lw, cw):
        acc = _dot(hb, w_ref[:, c0:c0 + cw])
        gate_ref[:, :, c0:c0 + cw] = jax.nn.gelu(acc).reshape(tt, nb, cw)
    for c0 in range(0, lw, cw):
        acc = _dot(hb, w_ref[:, lw + c0:lw + c0 + cw])
        rec_ref[:, :, c0:c0 + cw] = acc.reshape(tt, nb, cw)


def _lru_in_call(x3, modtab, gain, w, *, tt, ctx_len):
    ta, nb, d = x3.shape
    lw = w.shape[1] // 2
    nctx = ctx_len // tt
    kern = functools.partial(_lru_in_kernel, d=d, lw=lw)
    return pl.pallas_call(
        kern,
        out_shape=(jax.ShapeDtypeStruct((ta, nb, lw), _F32),
                   jax.ShapeDtypeStruct((ta, nb, lw), _F32)),
        grid=(ta // tt,),
        in_specs=[pl.BlockSpec((tt, nb, d), lambda i: (i, 0, 0)),
                  pl.BlockSpec((1, nb, 6 * d), lambda i: (jnp.where(i < nctx, 0, 1), 0, 0)),
                  pl.BlockSpec((1, d), lambda i: (0, 0)),
                  pl.BlockSpec((d, 2 * lw), lambda i: (0, 0))],
        out_specs=(pl.BlockSpec((tt, nb, lw), lambda i: (i, 0, 0)),
                   pl.BlockSpec((tt, nb, lw), lambda i: (i, 0, 0))),
        compiler_params=_cparams(1),
        name="lru_in",
    )(x3, modtab, gain, w)


def _attn_kernel(q_ref, k_ref, v_ref, lam_ref, sg_ref, o_ref,
                 s_ref, va_ref, q2_ref, mx_ref, mf_ref, oa_ref, *, tq, ctx_len, lambda_init):
    ta, hw = k_ref.shape
    hd = hw // 2
    n_ctx = ctx_len // tq
    n_lat = ta // tq - n_ctx
    nkb = ta // KB
    ctx_kb = ctx_len // KB
    lane = lax.broadcasted_iota(jnp.int32, (tq, hw), 1)
    lv = lam_ref[...]
    lam = (jnp.exp(jnp.sum(lv[0:1] * lv[1:2], axis=-1, keepdims=True))
           - jnp.exp(jnp.sum(lv[2:3] * lv[3:4], axis=-1, keepdims=True)) + lambda_init)
    out_scale = sg_ref[...] * (1.0 - lambda_init)

    va_ref[:, 0:hw] = v_ref[...]
    va_ref[:, hw:2 * hw] = jnp.ones((ta, hw), _BF)

    def stack_q(r0):
        q = q_ref[pl.ds(r0, tq), :]
        zero = jnp.zeros_like(q)
        return jnp.concatenate([jnp.where(lane < hd, q, zero), jnp.where(lane < hd, zero, q)], axis=0)

    def nt_dot(a, b):
        return lax.dot_general(a, b, (((1,), (1,)), ((), ())), preferred_element_type=_F32)

    def combine(acc):
        o1 = acc[0:tq, 0:hw] / acc[0:tq, hw:2 * hw]
        o2 = acc[tq:2 * tq, 0:hw] / acc[tq:2 * tq, hw:2 * hw]
        o = o1 - lam * o2
        return (_rms(o) * out_scale).astype(o_ref.dtype)

    for c in range(n_ctx):
        s = nt_dot(stack_q(c * tq), k_ref[0:ctx_len, :])
        e = jnp.exp2(s - jnp.max(s, axis=-1, keepdims=True))
        o_ref[c * tq:(c + 1) * tq, :] = combine(_dot(e.astype(_BF), va_ref[0:ctx_len, :]))

    def block_step(j0, nj, par, do_a, do_b):
        k0 = pl.multiple_of(j0 * KB, KB) if not isinstance(j0, int) else j0 * KB
        if do_a:
            sa = nt_dot(q2_ref[...], k_ref[pl.ds(k0, nj * KB), :])
            for rb in range(2 * tq // RB):
                rows = slice(rb * RB, (rb + 1) * RB)
                mx = mx_ref[par, rows, :]
                for u in range(nj):
                    blk = sa[rows, u * KB:(u + 1) * KB]
                    s_ref[par, j0 + u, rows, :] = blk
                    for l0 in range(0, KB, LANES):
                        mx = jnp.maximum(mx, blk[:, l0:l0 + LANES])
                mx_ref[par, rows, :] = mx
        if do_b:
            e_rows = []
            for rb in range(2 * tq // RB):
                rows = slice(rb * RB, (rb + 1) * RB)
                mf = mf_ref[1 - par, rows, :]
                parts = []
                for u in range(nj):
                    sb = s_ref[1 - par, j0 + u, rows, :]
                    for l0 in range(0, KB, LANES):
                        parts.append(jnp.exp2(sb[:, l0:l0 + LANES] - mf).astype(_BF))
                e_rows.append(jnp.concatenate(parts, axis=1))
            e = jnp.concatenate(e_rows, axis=0)
            oa_ref[...] += _dot(e, va_ref[pl.ds(k0, nj * KB), :])

    def sweep(tau, par, do_a, do_b):
        if do_a:
            r0 = (n_ctx + tau) * tq
            q2_ref[...] = stack_q(r0 if isinstance(r0, int) else pl.multiple_of(r0, tq))
            mx_ref[par] = jnp.full(mx_ref.shape[1:], -jnp.inf, _F32)
        if do_b:
            oa_ref[...] = jnp.zeros_like(oa_ref)
        step = functools.partial(block_step, par=par, do_a=do_a, do_b=do_b)
        for j in range(ctx_kb):
            step(j, 1)
        n_pairs = (nkb - ctx_kb) // 2
        assert (nkb - ctx_kb) % 2 == 0
        unroll = math.gcd(n_pairs, STEP_UNROLL)

        def body(c, carry):
            for u in range(unroll):
                step(ctx_kb + 2 * (unroll * c + u), 2)
            return carry

        if n_pairs == unroll:
            body(0, 0)
        else:
            lax.fori_loop(0, n_pairs // unroll, body, 0)
        if do_a:
            m = jnp.max(mx_ref[par], axis=-1, keepdims=True)
            mf_ref[par] = jnp.broadcast_to(m, mf_ref.shape[1:])
        if do_b:
            r0 = (n_ctx + tau - 1) * tq
            r0 = r0 if isinstance(r0, int) else pl.multiple_of(r0, tq)
            o_ref[pl.ds(r0, tq), :] = combine(oa_ref[...])

    assert n_lat >= 2 and n_lat % 2 == 0
    sweep(0, 0, True, False)

    def pair(p, carry):
        sweep(2 * p + 1, 1, True, True)
        sweep(2 * p + 2, 0, True, True)
        return carry

    lax.fori_loop(0, (n_lat - 2) // 2, pair, 0)
    sweep(n_lat - 1, 1, True, True)
    sweep(n_lat, 0, False, True)


def _attn_call(qkv, lam_vecs, subln, *, ctx_len, tq, lambda_init):
    nb, ta, n = qkv.shape
    d = n // 3
    hw = d // ATTN_HEADS
    assert ta % KB == 0 and ctx_len % KB == 0 and tq % RB == 0
    nkb = ta // KB
    kern = functools.partial(_attn_kernel, tq=tq, ctx_len=ctx_len, lambda_init=lambda_init)
    return pl.pallas_call(
        kern,
        out_shape=jax.ShapeDtypeStruct((nb, ta, d), _BF),
        grid=(nb, ATTN_HEADS),
        in_specs=[pl.BlockSpec((None, ta, hw), lambda b, h: (b, 0, h)),
                  pl.BlockSpec((None, ta, hw), lambda b, h: (b, 0, ATTN_HEADS + h)),
                  pl.BlockSpec((None, ta, hw), lambda b, h: (b, 0, 2 * ATTN_HEADS + h)),
                  pl.BlockSpec(lam_vecs.shape, lambda b, h: (0, 0)),
                  pl.BlockSpec((1, hw), lambda b, h: (0, 0))],
        out_specs=pl.BlockSpec((None, ta, hw), lambda b, h: (b, 0, h)),
        scratch_shapes=[pltpu.VMEM((2, nkb, 2 * tq, KB), _F32),
                        pltpu.VMEM((ta, 2 * hw), _BF),
                        pltpu.VMEM((2 * tq, hw), _BF),
                        pltpu.VMEM((2, 2 * tq, LANES), _F32),
                        pltpu.VMEM((2, 2 * tq, LANES), _F32),
                        pltpu.VMEM((2 * tq, 2 * hw), _F32)],
        compiler_params=_cparams(2),
        name="diff_attn",
    )(qkv, qkv, qkv, lam_vecs, subln)


def _mlp_kernel(x_ref, a_ref, wa_ref, modp_ref, modm_ref, ng_ref, w1_ref, w2_ref, o_ref,
                x1_ref, h2_ref, acc_ref, *, d, tf, glu, x_bm, a_bm, out_bm):
    _, tt, nb, _ = x1_ref.shape
    tm = tt * nb
    dff = w1_ref.shape[1]
    i = pl.program_id(0)
    slot = i % 2

    @pl.when(i == 0)
    def _():
        x1_ref[...] = jnp.zeros_like(x1_ref)
        h2_ref[...] = jnp.zeros_like(h2_ref)

    h2 = h2_ref[1 - slot]
    for f0 in range(0, dff, tf):
        t = _dot(h2, w1_ref[:, f0:f0 + tf])
        t = jnp.maximum(t, 0.0)
        t = (t * t).astype(_BF)
        if f0 == 0:
            acc_ref[...] = _dot(t, w2_ref[f0:f0 + tf, :])
        else:
            acc_ref[...] += _dot(t, w2_ref[f0:f0 + tf, :])
    modm = modm_ref[0]
    m3 = acc_ref[...].reshape(tt, nb, d)
    out = x1_ref[1 - slot] + modm[:, 5 * d:6 * d] * (_rms(m3) * ng_ref[3:4])
    o_ref[...] = pltpu.einshape("tbd->btd", out) if out_bm else out

    mod = modp_ref[0]
    a = a_ref[...]
    if a_bm:
        a = a.reshape(tm, a.shape[-1])
    y = _dot(a, wa_ref[...])
    if glu:
        y = y[:, 0:d] * jax.nn.sigmoid(y[:, d:2 * d])
    y3 = pltpu.einshape("btd->tbd", y.reshape(nb, tt, d)) if a_bm else y.reshape(tt, nb, d)
    x3 = pltpu.einshape("btd->tbd", x_ref[...]) if x_bm else x_ref[...]
    x1 = x3 + mod[:, 2 * d:3 * d] * (_rms(y3) * ng_ref[1:2])
    x1_ref[slot] = x1
    h2n = _normmod(x1, ng_ref[2:3], mod[:, 4 * d:5 * d], mod[:, 3 * d:4 * d])
    h2_ref[slot] = h2n.reshape(tm, d).astype(_BF)


def _mlp_call(x, a, wa, modtab, ng, w1, w2, *, tt, tf, ctx_len, glu, x_bm, a_bm, out_bm, skip_ctx):
    if x_bm:
        nb, ta, d = x.shape
    else:
        ta, nb, d = x.shape
    tm = tt * nb
    ka, na = wa.shape
    dff = w1.shape[1]
    nctx = ctx_len // tt
    t_off = nctx if skip_ctx else 0
    n_t = ta // tt - t_off

    def prep_tile(i):
        return jnp.minimum(i, n_t - 1) + t_off

    def done_tile(i):
        return jnp.maximum(i - 1, 0)

    def seg(tile):
        return jnp.where(tile < nctx, 0, 1)

    resident = dict(pipeline_mode=pl.Buffered(1))
    x_spec = (pl.BlockSpec((nb, tt, d), lambda i: (0, prep_tile(i), 0)) if x_bm
              else pl.BlockSpec((tt, nb, d), lambda i: (prep_tile(i), 0, 0)))
    a_spec = (pl.BlockSpec((nb, tt, ka), lambda i: (0, prep_tile(i), 0)) if a_bm
              else pl.BlockSpec((tm, ka), lambda i: (prep_tile(i), 0)))
    if out_bm:
        out_shape = jax.ShapeDtypeStruct((nb, n_t * tt, d), _F32)
        out_spec = pl.BlockSpec((nb, tt, d), lambda i: (0, done_tile(i), 0))
    else:
        out_shape = jax.ShapeDtypeStruct((n_t * tt, nb, d), _F32)
        out_spec = pl.BlockSpec((tt, nb, d), lambda i: (done_tile(i), 0, 0))
    kern = functools.partial(_mlp_kernel, d=d, tf=tf, glu=glu, x_bm=x_bm, a_bm=a_bm, out_bm=out_bm)
    return pl.pallas_call(
        kern,
        out_shape=out_shape,
        grid=(n_t + 1,),
        in_specs=[x_spec,
                  a_spec,
                  pl.BlockSpec((ka, na), lambda i: (0, 0), **resident),
                  pl.BlockSpec((1, nb, 6 * d), lambda i: (seg(prep_tile(i)), 0, 0)),
                  pl.BlockSpec((1, nb, 6 * d), lambda i: (seg(done_tile(i) + t_off), 0, 0)),
                  pl.BlockSpec((4, d), lambda i: (0, 0)),
                  pl.BlockSpec((d, dff), lambda i: (0, 0), **resident),
                  pl.BlockSpec((dff, d), lambda i: (0, 0), **resident)],
        out_specs=out_spec,
        scratch_shapes=[pltpu.VMEM((2, tt, nb, d), _F32),
                        pltpu.VMEM((2, tm, d), _BF),
                        pltpu.VMEM((tm, d), _F32)],
        compiler_params=_cparams(1),
        name="proj_mlp",
    )(x, a, wa, modtab, modtab, ng, w1, w2)


def _chunk_of(i, n_chunks, n_ctx_chunks, reverse):
    if not reverse:
        return i
    return jnp.where(i < n_ctx_chunks, n_ctx_chunks - 1 - i, n_chunks - 1 - (i - n_ctx_chunks))


def _s5_kernel(*refs, d, reverse, final):
    if final:
        (x_ref, mod_ref, g_ref, bw_ref, cw_ref, lam_ref, yb_ref, dsk_ref,
         o_ref, bu_ref, h_ref) = refs
    else:
        x_ref, mod_ref, g_ref, bw_ref, cw_ref, lam_ref, o_ref, bu_ref, h_ref = refs
    tt, nb, _ = x_ref.shape
    tm = tt * nb
    nslab = d // LANES
    sw = bw_ref.shape[2]
    hw = sw // 2

    @pl.when(pl.program_id(0) == 0)
    def _():
        h_ref[...] = jnp.zeros_like(h_ref)

    mod = mod_ref[0]
    u3 = _normmod(x_ref[...], g_ref[...], mod[:, d:2 * d], mod[:, 0:d])
    ub = u3.reshape(tm, d).astype(_BF)
    for j in range(nslab):
        bu = _dot(ub[:, j * LANES:(j + 1) * LANES], bw_ref[j])
        bu_ref[:, :, j * sw:(j + 1) * sw] = bu.reshape(tt, nb, sw)

    for j in range(nslab):
        cr = j * sw
        ci = cr + hw
        lr = lam_ref[j]
        li = lam_ref[nslab + j]

        def body(s, carry, cr=cr, ci=ci, lr=lr, li=li):
            hr, hi = carry
            idx = tt - 1 - s if reverse else s
            nr = lr * hr - li * hi + bu_ref[idx, :, cr:cr + hw]
            ni = lr * hi + li * hr + bu_ref[idx, :, ci:ci + hw]
            bu_ref[idx, :, cr:cr + hw] = nr
            bu_ref[idx, :, ci:ci + hw] = ni
            return nr, ni

        hr, hi = lax.fori_loop(0, tt, body, (h_ref[:, cr:cr + hw], h_ref[:, ci:ci + hw]),
                               unroll=4)
        h_ref[:, cr:cr + hw] = hr
        h_ref[:, ci:ci + hw] = hi

    for j in range(nslab):
        hs = bu_ref[:, :, j * sw:(j + 1) * sw].reshape(tm, sw).astype(_BF)
        y = _dot(hs, cw_ref[j]).reshape(tt, nb, LANES)
        cols = slice(j * LANES, (j + 1) * LANES)
        if final:
            tot = y + yb_ref[:, :, cols] + dsk_ref[:, cols] * u3[:, :, cols]
            o_ref[:, cols] = jax.nn.gelu(tot).reshape(tm, LANES).astype(o_ref.dtype)
        else:
            o_ref[:, :, cols] = y


def _s5_call(x3, modtab, gain, bw, cw, lam, yb, dskip, *, tt, ctx_len, reverse):
    ta, nb, d = x3.shape
    tm = tt * nb
    n_chunks = ta // tt
    nctx = ctx_len // tt
    final = yb is not None
    nslab, _, sw = bw.shape

    def cidx(i):
        return _chunk_of(i, n_chunks, nctx, reverse)

    in_specs = [pl.BlockSpec((tt, nb, d), lambda i: (cidx(i), 0, 0)),
                pl.BlockSpec((1, nb, 6 * d), lambda i: (jnp.where(cidx(i) < nctx, 0, 1), 0, 0)),
                pl.BlockSpec((1, d), lambda i: (0, 0)),
                pl.BlockSpec(bw.shape, lambda i: (0, 0, 0)),
                pl.BlockSpec(cw.shape, lambda i: (0, 0, 0)),
                pl.BlockSpec(lam.shape, lambda i: (0, 0, 0))]
    args = [x3, modtab, gain, bw, cw, lam]
    if final:
        in_specs += [pl.BlockSpec((tt, nb, d), lambda i: (cidx(i), 0, 0)),
                     pl.BlockSpec((1, d), lambda i: (0, 0))]
        args += [yb, dskip]
        out_shape = jax.ShapeDtypeStruct((ta * nb, d), _BF)
        out_spec = pl.BlockSpec((tm, d), lambda i: (cidx(i), 0))
    else:
        out_shape = jax.ShapeDtypeStruct((ta, nb, d), _F32)
        out_spec = pl.BlockSpec((tt, nb, d), lambda i: (cidx(i), 0, 0))
    kern = functools.partial(_s5_kernel, d=d, reverse=reverse, final=final)
    return pl.pallas_call(
        kern,
        out_shape=out_shape,
        grid=(n_chunks,),
        in_specs=in_specs,
        out_specs=out_spec,
        scratch_shapes=[pltpu.VMEM((tt, nb, nslab * sw), _F32),
                        pltpu.VMEM((nb, nslab * sw), _F32)],
        compiler_params=_cparams(1),
        name="s5_final" if final else "s5_first",
    )(*args)


def _s5_weights(a_re, a_im, b_re, b_im, c_re, c_im, log_dt, nb):
    ng, p = a_re.shape
    gc = b_re.shape[-1]
    gps = LANES // gc
    nslab = ng // gps
    lam = lax.complex(a_re.astype(_F32), a_im.astype(_F32))
    dt = jnp.exp(log_dt.astype(_F32))[:, None]
    lam_bar = jnp.exp(lam * dt)
    b_mat = lax.complex(b_re.astype(_F32), b_im.astype(_F32))
    b_bar = ((lam_bar - 1.0) / lam)[..., None] * b_mat
    eye = jnp.eye(gps, dtype=_F32)

    def b_block(part):
        blk = part.reshape(nslab, gps, p, gc)
        return jnp.einsum('jgpc,gh->jgchp', blk, eye).reshape(nslab, gps * gc, gps * p)

    bw = jnp.concatenate([b_block(jnp.real(b_bar)), b_block(jnp.imag(b_bar))], axis=-1)

    def c_block(part):
        blk = part.reshape(nslab, gps, gc, p)
        return jnp.einsum('jgcp,gh->jgphc', blk, eye).reshape(nslab, gps * p, gps * gc)

    cw = jnp.concatenate([c_block(c_re.astype(_F32)), c_block(-c_im.astype(_F32))], axis=1)

    def lam_rows(part):
        return jnp.broadcast_to(part.reshape(nslab, 1, gps * p), (nslab, nb, gps * p))

    lam_t = jnp.concatenate([lam_rows(jnp.real(lam_bar)), lam_rows(jnp.imag(lam_bar))], axis=0)
    return bw.astype(_BF), cw.astype(_BF), lam_t


def _lru_kernel(*refs, tt, n_chunks, n_ctx_chunks, reverse, final):
    if final:
        (rec_ref, prev_ref, next_ref, cw_ref, cb_ref, wg_ref, bg_ref, ap_ref, hb_ref, gate_ref,
         o_ref, a_ref, b_ref, h_ref) = refs
    else:
        (rec_ref, prev_ref, next_ref, cw_ref, cb_ref, wg_ref, bg_ref, ap_ref,
         o_ref, a_ref, b_ref, h_ref) = refs
    _, nb, lw = rec_ref.shape
    tm = tt * nb
    nblk, bwid, _ = wg_ref.shape
    i = pl.program_id(0)

    @pl.when(i == 0)
    def _():
        h_ref[...] = jnp.zeros_like(h_ref)

    c = _chunk_of(i, n_chunks, n_ctx_chunks, reverse)
    seg_start = jnp.logical_or(c == 0, c == n_ctx_chunks)
    seg_end = jnp.logical_or(c == n_ctx_chunks - 1, c == n_chunks - 1)
    prev = jnp.where(seg_start, 0.0, prev_ref[...])
    nxt = jnp.where(seg_end, 0.0, next_ref[...])
    ext = jnp.concatenate([prev, rec_ref[...], nxt], axis=0)
    u3 = cb_ref[...]
    for k in range(CONV_W):
        u3 = u3 + cw_ref[k:k + 1] * ext[k:k + tt]
    u = u3.reshape(tm, lw)
    ub = u.astype(_BF)
    sp = jax.nn.softplus(-ap_ref[...])
    for n in range(nblk):
        cols = slice(n * bwid, (n + 1) * bwid)
        g = _dot(ub[:, cols], wg_ref[n])
        r = jax.nn.sigmoid(g[:, 0:bwid] + bg_ref[0:1, cols])
        ig = jax.nn.sigmoid(g[:, bwid:2 * bwid] + bg_ref[1:2, cols])
        log_a = -LRU_C * r * sp[:, cols]
        a = jnp.exp(log_a)
        a_ref[:, :, cols] = a.reshape(tt, nb, bwid)
        bterm = jnp.sqrt(-jnp.tanh(log_a) * (1.0 + a * a)) * (ig * u[:, cols])
        b_ref[:, :, cols] = bterm.reshape(tt, nb, bwid)

    sc = 4 * LANES
    for c0 in range(0, lw, sc):

        def body(s, h, c0=c0):
            idx = tt - 1 - s if reverse else s
            h = a_ref[idx, :, c0:c0 + sc] * h + b_ref[idx, :, c0:c0 + sc]
            b_ref[idx, :, c0:c0 + sc] = h
            return h

        h_ref[:, c0:c0 + sc] = lax.fori_loop(0, tt, body, h_ref[:, c0:c0 + sc], unroll=4)

    if final:
        tot = (b_ref[...] + hb_ref[...]) * gate_ref[...]
        o_ref[...] = tot.reshape(tm, lw).astype(o_ref.dtype)
    else:
        o_ref[...] = b_ref[...]


def _lru_call(rec3, conv_w, conv_b, wg, bg, ap, hb, gate3, *, tt, ctx_len, reverse):
    ta, nb, lw = rec3.shape
    tm = tt * nb
    n_chunks = ta // tt
    nctx = ctx_len // tt
    final = hb is not None
    half = tt // 2

    def cidx(i):
        return _chunk_of(i, n_chunks, nctx, reverse)

    in_specs = [pl.BlockSpec((tt, nb, lw), lambda i: (cidx(i), 0, 0)),
                pl.BlockSpec((2, nb, lw), lambda i: (jnp.maximum(cidx(i) * half - 1, 0), 0, 0)),
                pl.BlockSpec((1, nb, lw), lambda i: (jnp.minimum((cidx(i) + 1) * tt, ta - 1), 0, 0)),
                pl.BlockSpec(conv_w.shape, lambda i: (0, 0)),
                pl.BlockSpec(conv_b.shape, lambda i: (0, 0)),
                pl.BlockSpec(wg.shape, lambda i: (0, 0, 0)),
                pl.BlockSpec(bg.shape, lambda i: (0, 0)),
                pl.BlockSpec(ap.shape, lambda i: (0, 0))]
    args = [rec3, rec3, rec3, conv_w, conv_b, wg, bg, ap]
    if final:
        in_specs += [pl.BlockSpec((tt, nb, lw), lambda i: (cidx(i), 0, 0)),
                     pl.BlockSpec((tt, nb, lw), lambda i: (cidx(i), 0, 0))]
        args += [hb, gate3]
        out_shape = jax.ShapeDtypeStruct((ta * nb, lw), _BF)
        out_spec = pl.BlockSpec((tm, lw), lambda i: (cidx(i), 0))
    else:
        out_shape = jax.ShapeDtypeStruct((ta, nb, lw), _F32)
        out_spec = pl.BlockSpec((tt, nb, lw), lambda i: (cidx(i), 0, 0))
    kern = functools.partial(_lru_kernel, tt=tt, n_chunks=n_chunks, n_ctx_chunks=nctx,
                             reverse=reverse, final=final)
    return pl.pallas_call(
        kern,
        out_shape=out_shape,
        grid=(n_chunks,),
        in_specs=in_specs,
        out_specs=out_spec,
        scratch_shapes=[pltpu.VMEM((tt, nb, lw), _F32),
                        pltpu.VMEM((tt, nb, lw), _F32),
                        pltpu.VMEM((nb, lw), _F32)],
        compiler_params=_cparams(1),
        name="lru_final" if final else "lru_first",
    )(*args)


def _rope_tables(seq, ctx_len, hd):
    n_freq = hd // 4
    pos = jnp.arange(seq)
    row = (pos // GRID_W).astype(_F32)
    col = (pos % GRID_W).astype(_F32)
    inv = ROPE_BASE ** (-jnp.arange(n_freq, dtype=_F32) / n_freq)
    ang = jnp.concatenate([row[:, None] * inv, col[:, None] * inv], axis=-1)
    cos, sin = jnp.cos(ang), jnp.sin(ang)
    cos_c = jnp.tile(jnp.concatenate([cos, cos], axis=-1), (1, LANES // hd))
    sin_c = jnp.tile(jnp.concatenate([-sin, sin], axis=-1), (1, LANES // hd))
    cos_c = jnp.concatenate([jnp.ones((ctx_len, LANES), _F32), cos_c], axis=0)
    sin_c = jnp.concatenate([jnp.zeros((ctx_len, LANES), _F32), sin_c], axis=0)
    return cos_c, sin_c


def kernel(x, c, ctx, c_ctx, ada_w, ada_b, norm_g, mlp_w1, mlp_w2, attn_w_qkv, attn_w_o, attn_lambda, attn_subln, s5_a_re, s5_a_im, s5_b_re, s5_b_im, s5_c_re, s5_c_im, s5_log_dt, s5_d, s5_w_glu, lru_w_in, lru_conv_w, lru_conv_b, lru_w_gate, lru_b_gate, lru_a_param, lru_w_out):
    nb, seq, d = x.shape
    ctx_len = ctx.shape[1]
    depth = ada_w.shape[0]
    assert nb % SUBLANES == 0 and d % LANES == 0
    assert depth % N_MIXERS == 1, "first and last layers must be attention layers (batch-major ends)"
    tt = 32
    tq = 256
    tf = 1024
    assert ctx_len % tq == 0 and seq % tq == 0 and ctx_len % tt == 0 and seq % tt == 0

    pad = (-(nb + 1)) % SUBLANES
    c_all = jnp.concatenate([c, c_ctx[None, :], jnp.zeros((pad, d), c.dtype)], axis=0)
    mod_all = _ada_call(c_all, ada_w.astype(_BF), ada_b)
    mod_lat = mod_all[:, :nb]
    mod_ctx = jnp.broadcast_to(mod_all[:, nb:nb + 1], mod_lat.shape)
    modtabs = jnp.stack([mod_ctx, mod_lat], axis=1)

    cos_t, sin_t = _rope_tables(seq, ctx_len, d // ATTN_HEADS // 2)
    w1b = mlp_w1.astype(_BF)
    w2b = mlp_w2.astype(_BF)

    xs = jnp.concatenate([ctx, x], axis=1)
    x_bm = True
    for i in range(depth):
        kind, j = i % N_MIXERS, i // N_MIXERS
        last = i == depth - 1
        modtab = modtabs[i]
        gain0 = norm_g[i, 0:1]
        if kind == 0:
            lambda_init = 0.8 - 0.6 * math.exp(-0.3 * i)
            qkv = _qkv_call(xs, modtab, gain0, attn_w_qkv[j].astype(_BF), cos_t, sin_t,
                            tt=tt, ctx_len=ctx_len, x_bm=x_bm)
            a = _attn_call(qkv, attn_lambda[j], attn_subln[j][None, :], ctx_len=ctx_len, tq=tq,
                           lambda_init=lambda_init)
            wa = attn_w_o[j].astype(_BF)
            glu, a_bm = False, True
        elif kind == 1:
            wts = [_s5_weights(s5_a_re[j, dr], s5_a_im[j, dr], s5_b_re[j, dr], s5_b_im[j, dr],
                               s5_c_re[j, dr], s5_c_im[j, dr], s5_log_dt[j, dr], nb)
                   for dr in range(2)]
            yb = _s5_call(xs, modtab, gain0, *wts[1], None, None, tt=tt, ctx_len=ctx_len,
                          reverse=True)
            a = _s5_call(xs, modtab, gain0, *wts[0], yb, s5_d[j][None, :], tt=tt,
                         ctx_len=ctx_len, reverse=False)
            wa = s5_w_glu[j].astype(_BF)
            glu, a_bm = True, False
        else:
            gate3, rec3 = _lru_in_call(xs, modtab, gain0, lru_w_in[j].astype(_BF), tt=tt,
                                       ctx_len=ctx_len)
            wg = [jnp.concatenate([lru_w_gate[j, dr, 0], lru_w_gate[j, dr, 1]], axis=-1).astype(_BF)
                  for dr in range(2)]
            cb = lru_conv_b[j][None, :]
            hb = _lru_call(rec3, lru_conv_w[j], cb, wg[1], lru_b_gate[j, 1],
                           lru_a_param[j, 1][None, :], None, None, tt=tt, ctx_len=ctx_len,
                           reverse=True)
            a = _lru_call(rec3, lru_conv_w[j], cb, wg[0], lru_b_gate[j, 0],
                          lru_a_param[j, 0][None, :], hb, gate3, tt=tt, ctx_len=ctx_len,
                          reverse=False)
            wa = lru_w_out[j].astype(_BF)
            glu, a_bm = False, False
        xs = _mlp_call(xs, a, wa, modtab, norm_g[i], w1b[i], w2b[i], tt=tt, tf=tf,
                       ctx_len=ctx_len, glu=glu, x_bm=x_bm, a_bm=a_bm, out_bm=last, skip_ctx=last)
        x_bm = last
    return xs
```

```python
import functools
import math

import jax
import jax.numpy as jnp
from jax import lax
from jax.experimental import pallas as pl
from jax.experimental.pallas import tpu as pltpu

EPS = 1e-6
GRID_W = 64
ROPE_BASE = 10000.0
N_MIXERS = 3
ATTN_HEADS = 8
S5_GROUP = 16
S5_STATE = 64
LRU_BW = 256
LRU_C = 8.0
CONV_W = 4
CONV_LEFT = 2

LANES = 128
SUBLANES = 8
VMEM_LIMIT = 56 * 1024 * 1024
MXU_TILE = 256

KB = MXU_TILE
RB = 8 * SUBLANES
STEP_UNROLL = 8
LOG2E = math.log2(math.e)

_BF = jnp.bfloat16
_F32 = jnp.float32


def _cparams(n_axes):
    return pltpu.CompilerParams(dimension_semantics=("arbitrary",) * n_axes,
                                vmem_limit_bytes=VMEM_LIMIT)


def _rms(x):
    return x * lax.rsqrt(jnp.mean(x * x, axis=-1, keepdims=True) + EPS)


def _normmod(x3, gain, scale, shift):
    return _rms(x3) * gain * (1.0 + scale) + shift


def _dot(a, b):
    return jnp.dot(a, b, preferred_element_type=_F32)


def _ada_kernel(c_ref, w_ref, b_ref, o_ref):
    sc = jax.nn.silu(c_ref[...]).astype(_BF)
    o_ref[0] = _dot(sc, w_ref[0]) + b_ref[0]


def _ada_call(c_all, ada_w, ada_b):
    depth, d, n = ada_w.shape
    rows = c_all.shape[0]
    tn = n // 4
    return pl.pallas_call(
        _ada_kernel,
        out_shape=jax.ShapeDtypeStruct((depth, rows, n), _F32),
        grid=(depth, n // tn),
        in_specs=[pl.BlockSpec((rows, d), lambda l, j: (0, 0)),
                  pl.BlockSpec((1, d, tn), lambda l, j: (l, 0, j)),
                  pl.BlockSpec((1, 1, tn), lambda l, j: (l, 0, j))],
        out_specs=pl.BlockSpec((1, rows, tn), lambda l, j: (l, 0, j)),
        compiler_params=_cparams(2),
        name="adaln",
    )(c_all, ada_w, ada_b.reshape(depth, 1, n))


def _rope_cols(t, cos, sin, lo_half):
    rot = jnp.where(lo_half, pltpu.roll(t, 96, 1), pltpu.roll(t, 32, 1))
    return t * cos + rot * sin


def _qkv_kernel(*refs, d, q_scale, n_ctx_tiles, split_in):
    if split_in:
        c_ref, x_ref, mod_ref, g_ref, w_ref, cos_ref, sin_ref, o_ref = refs
        xin = jnp.where(pl.program_id(0) < n_ctx_tiles, c_ref[...], x_ref[...])
        x3 = pltpu.einshape("btd->tbd", xin)
    else:
        x_ref, mod_ref, g_ref, w_ref, cos_ref, sin_ref, o_ref = refs
        x3 = x_ref[...]
    nb, tt, _ = o_ref.shape
    tm = tt * nb
    mod = mod_ref[0]
    h = _normmod(x3, g_ref[...], mod[:, d:2 * d], mod[:, 0:d])
    hb = pltpu.einshape("tbd->btd", h.astype(_BF)).reshape(tm, d)
    cos = jnp.broadcast_to(cos_ref[...][None], (nb, tt, LANES)).reshape(tm, LANES)
    sin = jnp.broadcast_to(sin_ref[...][None], (nb, tt, LANES)).reshape(tm, LANES)
    lane = lax.broadcasted_iota(jnp.int32, (tm, LANES), 1)
    lo_half = (lane % 64) < 32
    n_rope = 2 * d // LANES
    cw = 4 * LANES
    for c0 in range(0, 3 * d, cw):
        acc = _dot(hb, w_ref[:, c0:c0 + cw])
        for k in range(cw // LANES):
            col = c0 // LANES + k
            t = acc[:, k * LANES:(k + 1) * LANES]
            if col < n_rope:
                t = _rope_cols(t, cos, sin, lo_half)
                if col < n_rope // 2:
                    t = t * q_scale
            o_ref[:, :, col * LANES:(col + 1) * LANES] = t.reshape(nb, tt, LANES).astype(o_ref.dtype)


def _qkv_call(xs, modtab, gain, w, cos_t, sin_t, *, tt, ctx_len):
    split_in = isinstance(xs, tuple)
    nctx = ctx_len // tt
    if split_in:
        nb, seq, d = xs[1].shape
        ta = ctx_len + seq
        x_specs = [pl.BlockSpec((nb, tt, d), lambda i: (0, jnp.minimum(i, nctx - 1), 0)),
                   pl.BlockSpec((nb, tt, d), lambda i: (0, jnp.maximum(i - nctx, 0), 0))]
        x_args = list(xs)
    else:
        ta, nb, d = xs.shape
        x_specs = [pl.BlockSpec((tt, nb, d), lambda i: (i, 0, 0))]
        x_args = [xs]
    n = w.shape[1]
    q_scale = LOG2E * (d // ATTN_HEADS // 2) ** -0.5
    kern = functools.partial(_qkv_kernel, d=d, q_scale=q_scale, n_ctx_tiles=nctx, split_in=split_in)
    return pl.pallas_call(
        kern,
        out_shape=jax.ShapeDtypeStruct((nb, ta, n), _BF),
        grid=(ta // tt,),
        in_specs=x_specs + [
                  pl.BlockSpec((1, nb, 6 * d), lambda i: (jnp.where(i < nctx, 0, 1), 0, 0)),
                  pl.BlockSpec((1, d), lambda i: (0, 0)),
                  pl.BlockSpec((d, n), lambda i: (0, 0)),
                  pl.BlockSpec((tt, LANES), lambda i: (i, 0)),
                  pl.BlockSpec((tt, LANES), lambda i: (i, 0))],
        out_specs=pl.BlockSpec((nb, tt, n), lambda i: (0, i, 0)),
        compiler_params=_cparams(1),
        name="qkv_rope",
    )(*x_args, modtab, gain, w, cos_t, sin_t)


def _lru_in_kernel(x_ref, mod_ref, g_ref, w_ref, gate_ref, rec_ref, *, d, lw):
    tt, nb, _ = x_ref.shape
    tm = tt * nb
    mod = mod_ref[0]
    h = _normmod(x_ref[...], g_ref[...], mod[:, d:2 * d], mod[:, 0:d])
    hb = h.reshape(tm, d).astype(_BF)
    cw = 4 * LANES
    for c0 in range(0, lw, cw):
        acc = _dot(hb, w_ref[:, c0:c0 + cw])
        gate_ref[:, :, c0:c0 + cw] = jax.nn.gelu(acc).reshape(tt, nb, cw)
    for c0 in range(0, lw, cw):
        acc = _dot(hb, w_ref[:, lw + c0:lw + c0 + cw])
        rec_ref[:, :, c0:c0 + cw] = acc.reshape(tt, nb, cw)


def _lru_in_call(x3, modtab, gain, w, *, tt, ctx_len):
    ta, nb, d = x3.shape
    lw = w.shape[1] // 2
    nctx = ctx_len // tt
    kern = functools.partial(_lru_in_kernel, d=d, lw=lw)
    return pl.pallas_call(
        kern,
        out_shape=(jax.ShapeDtypeStruct((ta, nb, lw), _F32),
                   jax.ShapeDtypeStruct((ta, nb, lw), _F32)),
        grid=(ta // tt,),
        in_specs=[pl.BlockSpec((tt, nb, d), lambda i: (i, 0, 0)),
                  pl.BlockSpec((1, nb, 6 * d), lambda i: (jnp.where(i < nctx, 0, 1), 0, 0)),
                  pl.BlockSpec((1, d), lambda i: (0, 0)),
                  pl.BlockSpec((d, 2 * lw), lambda i: (0, 0))],
        out_specs=(pl.BlockSpec((tt, nb, lw), lambda i: (i, 0, 0)),
                   pl.BlockSpec((tt, nb, lw), lambda i: (i, 0, 0))),
        compiler_params=_cparams(1),
        name="lru_in",
    )(x3, modtab, gain, w)


def _attn_kernel(q_ref, k_ref, v_ref, lam_ref, sg_ref, o_ref,
                 s_ref, va_ref, q2_ref, mx_ref, mf_ref, oa_ref, *, tq, ctx_len, lambda_init):
    ta, hw = k_ref.shape
    hd = hw // 2
    n_ctx = ctx_len // tq
    n_lat = ta // tq - n_ctx
    nkb = ta // KB
    ctx_kb = ctx_len // KB
    lane = lax.broadcasted_iota(jnp.int32, (tq, hw), 1)
    lv = lam_ref[...]
    lam = (jnp.exp(jnp.sum(lv[0:1] * lv[1:2], axis=-1, keepdims=True))
           - jnp.exp(jnp.sum(lv[2:3] * lv[3:4], axis=-1, keepdims=True)) + lambda_init)
    out_scale = sg_ref[...] * (1.0 - lambda_init)

    va_ref[:, 0:hw] = v_ref[...]
    va_ref[:, hw:2 * hw] = jnp.ones((ta, hw), _BF)

    def stack_q(r0):
        q = q_ref[pl.ds(r0, tq), :]
        zero = jnp.zeros_like(q)
        return jnp.concatenate([jnp.where(lane < hd, q, zero), jnp.where(lane < hd, zero, q)], axis=0)

    def nt_dot(a, b):
        return lax.dot_general(a, b, (((1,), (1,)), ((), ())), preferred_element_type=_F32)

    def combine(acc):
        o1 = acc[0:tq, 0:hw] / acc[0:tq, hw:2 * hw]
        o2 = acc[tq:2 * tq, 0:hw] / acc[tq:2 * tq, hw:2 * hw]
        o = o1 - lam * o2
        return (_rms(o) * out_scale).astype(o_ref.dtype)

    for c in range(n_ctx):
        s = nt_dot(stack_q(c * tq), k_ref[0:ctx_len, :])
        e = jnp.exp2(s - jnp.max(s, axis=-1, keepdims=True))
        o_ref[c * tq:(c + 1) * tq, :] = combine(_dot(e.astype(_BF), va_ref[0:ctx_len, :]))

    def block_step(j0, nj, par, do_a, do_b):
        k0 = pl.multiple_of(j0 * KB, KB) if not isinstance(j0, int) else j0 * KB
        if do_a:
            sa = nt_dot(q2_ref[...], k_ref[pl.ds(k0, nj * KB), :])
            for rb in range(2 * tq // RB):
                rows = slice(rb * RB, (rb + 1) * RB)
                mx = mx_ref[par, rows, :]
                for u in range(nj):
                    blk = sa[rows, u * KB:(u + 1) * KB]
                    s_ref[par, j0 + u, rows, :] = blk
                    for l0 in range(0, KB, LANES):
                        mx = jnp.maximum(mx, blk[:, l0:l0 + LANES])
                mx_ref[par, rows, :] = mx
        if do_b:
            e_rows = []
            for rb in range(2 * tq // RB):
                rows = slice(rb * RB, (rb + 1) * RB)
                mf = mf_ref[1 - par, rows, :]
                parts = []
                for u in range(nj):
                    sb = s_ref[1 - par, j0 + u, rows, :]
                    for l0 in range(0, KB, LANES):
                        parts.append(jnp.exp2(sb[:, l0:l0 + LANES] - mf).astype(_BF))
                e_rows.append(jnp.concatenate(parts, axis=1))
            e = jnp.concatenate(e_rows, axis=0)
            oa_ref[...] += _dot(e, va_ref[pl.ds(k0, nj * KB), :])

    def sweep(tau, par, do_a, do_b):
        if do_a:
            r0 = (n_ctx + tau) * tq
            q2_ref[...] = stack_q(r0 if isinstance(r0, int) else pl.multiple_of(r0, tq))
            mx_ref[par] = jnp.full(mx_ref.shape[1:], -jnp.inf, _F32)
        if do_b:
            oa_ref[...] = jnp.zeros_like(oa_ref)
        step = functools.partial(block_step, par=par, do_a=do_a, do_b=do_b)
        for j in range(ctx_kb):
            step(j, 1)
        n_pairs = (nkb - ctx_kb) // 2
        assert (nkb - ctx_kb) % 2 == 0
        unroll = math.gcd(n_pairs, STEP_UNROLL)

        def body(c, carry):
            for u in range(unroll):
                step(ctx_kb + 2 * (unroll * c + u), 2)
            return carry

        if n_pairs == unroll:
            body(0, 0)
        else:
            lax.fori_loop(0, n_pairs // unroll, body, 0)
        if do_a:
            m = jnp.max(mx_ref[par], axis=-1, keepdims=True)
            mf_ref[par] = jnp.broadcast_to(m, mf_ref.shape[1:])
        if do_b:
            r0 = (n_ctx + tau - 1) * tq
            r0 = r0 if isinstance(r0, int) else pl.multiple_of(r0, tq)
            o_ref[pl.ds(r0, tq), :] = combine(oa_ref[...])

    assert n_lat >= 2 and n_lat % 2 == 0
    sweep(0, 0, True, False)

    def pair(p, carry):
        sweep(2 * p + 1, 1, True, True)
        sweep(2 * p + 2, 0, True, True)
        return carry

    lax.fori_loop(0, (n_lat - 2) // 2, pair, 0)
    sweep(n_lat - 1, 1, True, True)
    sweep(n_lat, 0, False, True)


def _attn_call(qkv, lam_vecs, subln, *, ctx_len, tq, lambda_init):
    nb, ta, n = qkv.shape
    d = n // 3
    hw = d // ATTN_HEADS
    assert ta % KB == 0 and ctx_len % KB == 0 and tq % RB == 0
    nkb = ta // KB
    kern = functools.partial(_attn_kernel, tq=tq, ctx_len=ctx_len, lambda_init=lambda_init)
    return pl.pallas_call(
        kern,
        out_shape=jax.ShapeDtypeStruct((nb, ta, d), _BF),
        grid=(nb, ATTN_HEADS),
        in_specs=[pl.BlockSpec((None, ta, hw), lambda b, h: (b, 0, h)),
                  pl.BlockSpec((None, ta, hw), lambda b, h: (b, 0, ATTN_HEADS + h)),
                  pl.BlockSpec((None, ta, hw), lambda b, h: (b, 0, 2 * ATTN_HEADS + h)),
                  pl.BlockSpec(lam_vecs.shape, lambda b, h: (0, 0)),
                  pl.BlockSpec((1, hw), lambda b, h: (0, 0))],
        out_specs=pl.BlockSpec((None, ta, hw), lambda b, h: (b, 0, h)),
        scratch_shapes=[pltpu.VMEM((2, nkb, 2 * tq, KB), _F32),
                        pltpu.VMEM((ta, 2 * hw), _BF),
                        pltpu.VMEM((2 * tq, hw), _BF),
                        pltpu.VMEM((2, 2 * tq, LANES), _F32),
                        pltpu.VMEM((2, 2 * tq, LANES), _F32),
                        pltpu.VMEM((2 * tq, 2 * hw), _F32)],
        compiler_params=_cparams(2),
        name="diff_attn",
    )(qkv, qkv, qkv, lam_vecs, subln)


def _mlp_kernel(*refs, d, tf, glu, n_ctx_tiles, n_tiles, split_in, a_bm, out_bm):
    if split_in:
        c_ref = refs[0]
        refs = refs[1:]
    (x_ref, a_ref, wa_ref, modp_ref, modm_ref, ng_ref, w1_ref, w2_ref, o_ref,
     x1_ref, h2_ref, acc_ref) = refs
    _, tt, nb, _ = x1_ref.shape
    tm = tt * nb
    dff = w1_ref.shape[1]
    i = pl.program_id(0)
    slot = i % 2

    @pl.when(i == 0)
    def _():
        x1_ref[...] = jnp.zeros_like(x1_ref)
        h2_ref[...] = jnp.zeros_like(h2_ref)

    h2 = h2_ref[1 - slot]
    for f0 in range(0, dff, tf):
        t = _dot(h2, w1_ref[:, f0:f0 + tf])
        t = jnp.maximum(t, 0.0)
        t = (t * t).astype(_BF)
        if f0 == 0:
            acc_ref[...] = _dot(t, w2_ref[f0:f0 + tf, :])
        else:
            acc_ref[...] += _dot(t, w2_ref[f0:f0 + tf, :])
    modm = modm_ref[0]
    m3 = acc_ref[...].reshape(tt, nb, d)
    out = x1_ref[1 - slot] + modm[:, 5 * d:6 * d] * (_rms(m3) * ng_ref[3:4])
    o_ref[...] = pltpu.einshape("tbd->btd", out) if out_bm else out

    mod = modp_ref[0]
    a = a_ref[...]
    if a_bm:
        a = a.reshape(tm, a.shape[-1])
    y = _dot(a, wa_ref[...])
    if glu:
        y = y[:, 0:d] * jax.nn.sigmoid(y[:, d:2 * d])
    y3 = pltpu.einshape("btd->tbd", y.reshape(nb, tt, d)) if a_bm else y.reshape(tt, nb, d)
    if split_in:
        xin = jnp.where(jnp.minimum(i, n_tiles - 1) < n_ctx_tiles, c_ref[...], x_ref[...])
        x3 = pltpu.einshape("btd->tbd", xin)
    else:
        x3 = x_ref[...]
    x1 = x3 + mod[:, 2 * d:3 * d] * (_rms(y3) * ng_ref[1:2])
    x1_ref[slot] = x1
    h2n = _normmod(x1, ng_ref[2:3], mod[:, 4 * d:5 * d], mod[:, 3 * d:4 * d])
    h2_ref[slot] = h2n.reshape(tm, d).astype(_BF)


def _mlp_call(x, a, wa, modtab, ng, w1, w2, *, tt, tf, ctx_len, glu, a_bm, out_bm, skip_ctx):
    split_in = isinstance(x, tuple)
    if split_in:
        nb, seq, d = x[1].shape
        ta = ctx_len + seq
    else:
        ta, nb, d = x.shape
    tm = tt * nb
    ka, na = wa.shape
    dff = w1.shape[1]
    nctx = ctx_len // tt
    t_off = nctx if skip_ctx else 0
    n_t = ta // tt - t_off

    def prep_tile(i):
        return jnp.minimum(i, n_t - 1) + t_off

    def done_tile(i):
        return jnp.maximum(i - 1, 0)

    def seg(tile):
        return jnp.where(tile < nctx, 0, 1)

    resident = dict(pipeline_mode=pl.Buffered(1))
    if split_in:
        assert not skip_ctx
        x_specs = [pl.BlockSpec((nb, tt, d), lambda i: (0, jnp.minimum(prep_tile(i), nctx - 1), 0)),
                   pl.BlockSpec((nb, tt, d), lambda i: (0, jnp.maximum(prep_tile(i) - nctx, 0), 0))]
        x_args = list(x)
    else:
        x_specs = [pl.BlockSpec((tt, nb, d), lambda i: (prep_tile(i), 0, 0))]
        x_args = [x]
    a_spec = (pl.BlockSpec((nb, tt, ka), lambda i: (0, prep_tile(i), 0)) if a_bm
              else pl.BlockSpec((tm, ka), lambda i: (prep_tile(i), 0)))
    if out_bm:
        out_shape = jax.ShapeDtypeStruct((nb, n_t * tt, d), _F32)
        out_spec = pl.BlockSpec((nb, tt, d), lambda i: (0, done_tile(i), 0))
    else:
        out_shape = jax.ShapeDtypeStruct((n_t * tt, nb, d), _F32)
        out_spec = pl.BlockSpec((tt, nb, d), lambda i: (done_tile(i), 0, 0))
    kern = functools.partial(_mlp_kernel, d=d, tf=tf, glu=glu, n_ctx_tiles=nctx, n_tiles=n_t,
                             split_in=split_in, a_bm=a_bm, out_bm=out_bm)
    return pl.pallas_call(
        kern,
        out_shape=out_shape,
        grid=(n_t + 1,),
        in_specs=x_specs + [
                  a_spec,
                  pl.BlockSpec((ka, na), lambda i: (0, 0), **resident),
                  pl.BlockSpec((1, nb, 6 * d), lambda i: (seg(prep_tile(i)), 0, 0)),
                  pl.BlockSpec((1, nb, 6 * d), lambda i: (seg(done_tile(i) + t_off), 0, 0)),
                  pl.BlockSpec((4, d), lambda i: (0, 0)),
                  pl.BlockSpec((d, dff), lambda i: (0, 0), **resident),
                  pl.BlockSpec((dff, d), lambda i: (0, 0), **resident)],
        out_specs=out_spec,
        scratch_shapes=[pltpu.VMEM((2, tt, nb, d), _F32),
                        pltpu.VMEM((2, tm, d), _BF),
                        pltpu.VMEM((tm, d), _F32)],
        compiler_params=_cparams(1),
        name="proj_mlp",
    )(*x_args, a, wa, modtab, modtab, ng, w1, w2)


def _chunk_of(i, n_chunks, n_ctx_chunks, reverse):
    if not reverse:
        return i
    return jnp.where(i < n_ctx_chunks, n_ctx_chunks - 1 - i, n_chunks - 1 - (i - n_ctx_chunks))


def _s5_kernel(*refs, d, reverse, final):
    if final:
        (x_ref, mod_ref, g_ref, bw_ref, cw_ref, lam_ref, yb_ref, dsk_ref,
         o_ref, bu_ref, h_ref) = refs
    else:
        x_ref, mod_ref, g_ref, bw_ref, cw_ref, lam_ref, o_ref, bu_ref, h_ref = refs
    tt, nb, _ = x_ref.shape
    tm = tt * nb
    nslab = d // LANES
    sw = bw_ref.shape[2]
    hw = sw // 2

    @pl.when(pl.program_id(0) == 0)
    def _():
        h_ref[...] = jnp.zeros_like(h_ref)

    mod = mod_ref[0]
    u3 = _normmod(x_ref[...], g_ref[...], mod[:, d:2 * d], mod[:, 0:d])
    ub = u3.reshape(tm, d).astype(_BF)
    for j in range(nslab):
        bu = _dot(ub[:, j * LANES:(j + 1) * LANES], bw_ref[j])
        bu_ref[:, :, j * sw:(j + 1) * sw] = bu.reshape(tt, nb, sw)

    for j in range(nslab):
        cr = j * sw
        ci = cr + hw
        lr = lam_ref[j]
        li = lam_ref[nslab + j]

        def body(s, carry, cr=cr, ci=ci, lr=lr, li=li):
            hr, hi = carry
            idx = tt - 1 - s if reverse else s
            nr = lr * hr - li * hi + bu_ref[idx, :, cr:cr + hw]
            ni = lr * hi + li * hr + bu_ref[idx, :, ci:ci + hw]
            bu_ref[idx, :, cr:cr + hw] = nr
            bu_ref[idx, :, ci:ci + hw] = ni
            return nr, ni

        hr, hi = lax.fori_loop(0, tt, body, (h_ref[:, cr:cr + hw], h_ref[:, ci:ci + hw]),
                               unroll=4)
        h_ref[:, cr:cr + hw] = hr
        h_ref[:, ci:ci + hw] = hi

    for j in range(nslab):
        hs = bu_ref[:, :, j * sw:(j + 1) * sw].reshape(tm, sw).astype(_BF)
        y = _dot(hs, cw_ref[j]).reshape(tt, nb, LANES)
        cols = slice(j * LANES, (j + 1) * LANES)
        if final:
            tot = y + yb_ref[:, :, cols] + dsk_ref[:, cols] * u3[:, :, cols]
            o_ref[:, cols] = jax.nn.gelu(tot).reshape(tm, LANES).astype(o_ref.dtype)
        else:
            o_ref[:, :, cols] = y


def _s5_call(x3, modtab, gain, bw, cw, lam, yb, dskip, *, tt, ctx_len, reverse):
    ta, nb, d = x3.shape
    tm = tt * nb
    n_chunks = ta // tt
    nctx = ctx_len // tt
    final = yb is not None
    nslab, _, sw = bw.shape

    def cidx(i):
        return _chunk_of(i, n_chunks, nctx, reverse)

    in_specs = [pl.BlockSpec((tt, nb, d), lambda i: (cidx(i), 0, 0)),
                pl.BlockSpec((1, nb, 6 * d), lambda i: (jnp.where(cidx(i) < nctx, 0, 1), 0, 0)),
                pl.BlockSpec((1, d), lambda i: (0, 0)),
                pl.BlockSpec(bw.shape, lambda i: (0, 0, 0)),
                pl.BlockSpec(cw.shape, lambda i: (0, 0, 0)),
                pl.BlockSpec(lam.shape, lambda i: (0, 0, 0))]
    args = [x3, modtab, gain, bw, cw, lam]
    if final:
        in_specs += [pl.BlockSpec((tt, nb, d), lambda i: (cidx(i), 0, 0)),
                     pl.BlockSpec((1, d), lambda i: (0, 0))]
        args += [yb, dskip]
        out_shape = jax.ShapeDtypeStruct((ta * nb, d), _BF)
        out_spec = pl.BlockSpec((tm, d), lambda i: (cidx(i), 0))
    else:
        out_shape = jax.ShapeDtypeStruct((ta, nb, d), _F32)
        out_spec = pl.BlockSpec((tt, nb, d), lambda i: (cidx(i), 0, 0))
    kern = functools.partial(_s5_kernel, d=d, reverse=reverse, final=final)
    return pl.pallas_call(
        kern,
        out_shape=out_shape,
        grid=(n_chunks,),
        in_specs=in_specs,
        out_specs=out_spec,
        scratch_shapes=[pltpu.VMEM((tt, nb, nslab * sw), _F32),
                        pltpu.VMEM((nb, nslab * sw), _F32)],
        compiler_params=_cparams(1),
        name="s5_final" if final else "s5_first",
    )(*args)


def _s5_weights(a_re, a_im, b_re, b_im, c_re, c_im, log_dt, nb):
    ng, p = a_re.shape
    gc = b_re.shape[-1]
    gps = LANES // gc
    nslab = ng // gps
    lam = lax.complex(a_re.astype(_F32), a_im.astype(_F32))
    dt = jnp.exp(log_dt.astype(_F32))[:, None]
    lam_bar = jnp.exp(lam * dt)
    b_mat = lax.complex(b_re.astype(_F32), b_im.astype(_F32))
    b_bar = ((lam_bar - 1.0) / lam)[..., None] * b_mat
    eye = jnp.eye(gps, dtype=_F32)

    def b_block(part):
        blk = part.reshape(nslab, gps, p, gc)
        return jnp.einsum('jgpc,gh->jgchp', blk, eye).reshape(nslab, gps * gc, gps * p)

    bw = jnp.concatenate([b_block(jnp.real(b_bar)), b_block(jnp.imag(b_bar))], axis=-1)

    def c_block(part):
        blk = part.reshape(nslab, gps, gc, p)
        return jnp.einsum('jgcp,gh->jgphc', blk, eye).reshape(nslab, gps * p, gps * gc)

    cw = jnp.concatenate([c_block(c_re.astype(_F32)), c_block(-c_im.astype(_F32))], axis=1)

    def lam_rows(part):
        return jnp.broadcast_to(part.reshape(nslab, 1, gps * p), (nslab, nb, gps * p))

    lam_t = jnp.concatenate([lam_rows(jnp.real(lam_bar)), lam_rows(jnp.imag(lam_bar))], axis=0)
    return bw.astype(_BF), cw.astype(_BF), lam_t


def _lru_kernel(*refs, tt, n_chunks, n_ctx_chunks, reverse, final):
    if final:
        (rec_ref, prev_ref, next_ref, cw_ref, cb_ref, wg_ref, bg_ref, ap_ref, hb_ref, gate_ref,
         o_ref, a_ref, b_ref, h_ref) = refs
    else:
        (rec_ref, prev_ref, next_ref, cw_ref, cb_ref, wg_ref, bg_ref, ap_ref,
         o_ref, a_ref, b_ref, h_ref) = refs
    _, nb, lw = rec_ref.shape
    tm = tt * nb
    nblk, bwid, _ = wg_ref.shape
    i = pl.program_id(0)

    @pl.when(i == 0)
    def _():
        h_ref[...] = jnp.zeros_like(h_ref)

    c = _chunk_of(i, n_chunks, n_ctx_chunks, reverse)
    seg_start = jnp.logical_or(c == 0, c == n_ctx_chunks)
    seg_end = jnp.logical_or(c == n_ctx_chunks - 1, c == n_chunks - 1)
    prev = jnp.where(seg_start, 0.0, prev_ref[...])
    nxt = jnp.where(seg_end, 0.0, next_ref[...])
    ext = jnp.concatenate([prev, rec_ref[...], nxt], axis=0)
    u3 = cb_ref[...]
    for k in range(CONV_W):
        u3 = u3 + cw_ref[k:k + 1] * ext[k:k + tt]
    u = u3.reshape(tm, lw)
    ub = u.astype(_BF)
    sp = jax.nn.softplus(-ap_ref[...])
    for n in range(nblk):
        cols = slice(n * bwid, (n + 1) * bwid)
        g = _dot(ub[:, cols], wg_ref[n])
        r = jax.nn.sigmoid(g[:, 0:bwid] + bg_ref[0:1, cols])
        ig = jax.nn.sigmoid(g[:, bwid:2 * bwid] + bg_ref[1:2, cols])
        log_a = -LRU_C * r * sp[:, cols]
        a = jnp.exp(log_a)
        a_ref[:, :, cols] = a.reshape(tt, nb, bwid)
        bterm = jnp.sqrt(-jnp.tanh(log_a) * (1.0 + a * a)) * (ig * u[:, cols])
        b_ref[:, :, cols] = bterm.reshape(tt, nb, bwid)

    sc = 4 * LANES
    for c0 in range(0, lw, sc):

        def body(s, h, c0=c0):
            idx = tt - 1 - s if reverse else s
            h = a_ref[idx, :, c0:c0 + sc] * h + b_ref[idx, :, c0:c0 + sc]
            b_ref[idx, :, c0:c0 + sc] = h
            return h

        h_ref[:, c0:c0 + sc] = lax.fori_loop(0, tt, body, h_ref[:, c0:c0 + sc], unroll=4)

    if final:
        tot = (b_ref[...] + hb_ref[...]) * gate_ref[...]
        o_ref[...] = tot.reshape(tm, lw).astype(o_ref.dtype)
    else:
        o_ref[...] = b_ref[...]


def _lru_call(rec3, conv_w, conv_b, wg, bg, ap, hb, gate3, *, tt, ctx_len, reverse):
    ta, nb, lw = rec3.shape
    tm = tt * nb
    n_chunks = ta // tt
    nctx = ctx_len // tt
    final = hb is not None
    half = tt // 2

    def cidx(i):
        return _chunk_of(i, n_chunks, nctx, reverse)

    in_specs = [pl.BlockSpec((tt, nb, lw), lambda i: (cidx(i), 0, 0)),
                pl.BlockSpec((2, nb, lw), lambda i: (jnp.maximum(cidx(i) * half - 1, 0), 0, 0)),
                pl.BlockSpec((1, nb, lw), lambda i: (jnp.minimum((cidx(i) + 1) * tt, ta - 1), 0, 0)),
                pl.BlockSpec(conv_w.shape, lambda i: (0, 0)),
                pl.BlockSpec(conv_b.shape, lambda i: (0, 0)),
                pl.BlockSpec(wg.shape, lambda i: (0, 0, 0)),
                pl.BlockSpec(bg.shape, lambda i: (0, 0)),
                pl.BlockSpec(ap.shape, lambda i: (0, 0))]
    args = [rec3, rec3, rec3, conv_w, conv_b, wg, bg, ap]
    if final:
        in_specs += [pl.BlockSpec((tt, nb, lw), lambda i: (cidx(i), 0, 0)),
                     pl.BlockSpec((tt, nb, lw), lambda i: (cidx(i), 0, 0))]
        args += [hb, gate3]
        out_shape = jax.ShapeDtypeStruct((ta * nb, lw), _BF)
        out_spec = pl.BlockSpec((tm, lw), lambda i: (cidx(i), 0))
    else:
        out_shape = jax.ShapeDtypeStruct((ta, nb, lw), _F32)
        out_spec = pl.BlockSpec((tt, nb, lw), lambda i: (cidx(i), 0, 0))
    kern = functools.partial(_lru_kernel, tt=tt, n_chunks=n_chunks, n_ctx_chunks=nctx,
                             reverse=reverse, final=final)
    return pl.pallas_call(
        kern,
        out_shape=out_shape,
        grid=(n_chunks,),
        in_specs=in_specs,
        out_specs=out_spec,
        scratch_shapes=[pltpu.VMEM((tt, nb, lw), _F32),
                        pltpu.VMEM((tt, nb, lw), _F32),
                        pltpu.VMEM((nb, lw), _F32)],
        compiler_params=_cparams(1),
        name="lru_final" if final else "lru_first",
    )(*args)


def _rope_tables(seq, ctx_len, hd):
    n_freq = hd // 4
    pos = jnp.arange(seq)
    row = (pos // GRID_W).astype(_F32)
    col = (pos % GRID_W).astype(_F32)
    inv = ROPE_BASE ** (-jnp.arange(n_freq, dtype=_F32) / n_freq)
    ang = jnp.concatenate([row[:, None] * inv, col[:, None] * inv], axis=-1)
    cos, sin = jnp.cos(ang), jnp.sin(ang)
    cos_c = jnp.tile(jnp.concatenate([cos, cos], axis=-1), (1, LANES // hd))
    sin_c = jnp.tile(jnp.concatenate([-sin, sin], axis=-1), (1, LANES // hd))
    cos_c = jnp.concatenate([jnp.ones((ctx_len, LANES), _F32), cos_c], axis=0)
    sin_c = jnp.concatenate([jnp.zeros((ctx_len, LANES), _F32), sin_c], axis=0)
    return cos_c, sin_c


def kernel(x, c, ctx, c_ctx, ada_w, ada_b, norm_g, mlp_w1, mlp_w2, attn_w_qkv, attn_w_o, attn_lambda, attn_subln, s5_a_re, s5_a_im, s5_b_re, s5_b_im, s5_c_re, s5_c_im, s5_log_dt, s5_d, s5_w_glu, lru_w_in, lru_conv_w, lru_conv_b, lru_w_gate, lru_b_gate, lru_a_param, lru_w_out):
    nb, seq, d = x.shape
    ctx_len = ctx.shape[1]
    depth = ada_w.shape[0]
    assert nb % SUBLANES == 0 and d % LANES == 0
    assert depth % N_MIXERS == 1, "first and last layers must be attention layers (batch-major ends)"
    tt = 32
    tq = 256
    tf = 1024
    assert ctx_len % tq == 0 and seq % tq == 0 and ctx_len % tt == 0 and seq % tt == 0

    pad = (-(nb + 1)) % SUBLANES
    c_all = jnp.concatenate([c, c_ctx[None, :], jnp.zeros((pad, d), c.dtype)], axis=0)
    mod_all = _ada_call(c_all, ada_w.astype(_BF), ada_b)
    mod_lat = mod_all[:, :nb]
    mod_ctx = jnp.broadcast_to(mod_all[:, nb:nb + 1], mod_lat.shape)
    modtabs = jnp.stack([mod_ctx, mod_lat], axis=1)

    cos_t, sin_t = _rope_tables(seq, ctx_len, d // ATTN_HEADS // 2)
    w1b = mlp_w1.astype(_BF)
    w2b = mlp_w2.astype(_BF)

    xs = (ctx, x)
    for i in range(depth):
        kind, j = i % N_MIXERS, i // N_MIXERS
        last = i == depth - 1
        modtab = modtabs[i]
        gain0 = norm_g[i, 0:1]
        if kind == 0:
            lambda_init = 0.8 - 0.6 * math.exp(-0.3 * i)
            qkv = _qkv_call(xs, modtab, gain0, attn_w_qkv[j].astype(_BF), cos_t, sin_t,
                            tt=tt, ctx_len=ctx_len)
            a = _attn_call(qkv, attn_lambda[j], attn_subln[j][None, :], ctx_len=ctx_len, tq=tq,
                           lambda_init=lambda_init)
            wa = attn_w_o[j].astype(_BF)
            glu, a_bm = False, True
        elif kind == 1:
            wts = [_s5_weights(s5_a_re[j, dr], s5_a_im[j, dr], s5_b_re[j, dr], s5_b_im[j, dr],
                               s5_c_re[j, dr], s5_c_im[j, dr], s5_log_dt[j, dr], nb)
                   for dr in range(2)]
            yb = _s5_call(xs, modtab, gain0, *wts[1], None, None, tt=tt, ctx_len=ctx_len,
                          reverse=True)
            a = _s5_call(xs, modtab, gain0, *wts[0], yb, s5_d[j][None, :], tt=tt,
                         ctx_len=ctx_len, reverse=False)
            wa = s5_w_glu[j].astype(_BF)
            glu, a_bm = True, False
        else:
            gate3, rec3 = _lru_in_call(xs, modtab, gain0, lru_w_in[j].astype(_BF), tt=tt,
                                       ctx_len=ctx_len)
            wg = [jnp.concatenate([lru_w_gate[j, dr, 0], lru_w_gate[j, dr, 1]], axis=-1).astype(_BF)
                  for dr in range(2)]
            cb = lru_conv_b[j][None, :]
            hb = _lru_call(rec3, lru_conv_w[j], cb, wg[1], lru_b_gate[j, 1],
                           lru_a_param[j, 1][None, :], None, None, tt=tt, ctx_len=ctx_len,
                           reverse=True)
            a = _lru_call(rec3, lru_conv_w[j], cb, wg[0], lru_b_gate[j, 0],
                          lru_a_param[j, 0][None, :], hb, gate3, tt=tt, ctx_len=ctx_len,
                          reverse=False)
            wa = lru_w_out[j].astype(_BF)
            glu, a_bm = False, False
        xs = _mlp_call(xs, a, wa, modtab, norm_g[i], w1b[i], w2b[i], tt=tt, tf=tf,
                       ctx_len=ctx_len, glu=glu, a_bm=a_bm, out_bm=last, skip_ctx=last)
    return xs
```
